```python
import functools
import jax, jax.numpy as jnp
from jax import lax
import numpy as np

D_MODEL = 1024
BATCH = 4
SEQ = 8192
DEPTH = 2
DEC_BATCH = 128
DEC_SEQ = 4
PAST_LEN = 16384
PAGE_SIZE = 128

FOX_HEADS = 4
FOX_HEAD_DIM = 64
FOX_FORGET_BIAS = 4.0
MLSTM_HEADS = 4
MLSTM_DK = 128
MLSTM_DV = 128
MLSTM_CHUNK = 128
MLSTM_FORGET_BIAS = 3.0
MLSTM_INPUT_BIAS = -1.0
MLA_HEADS = 4
MLA_Q_LORA = 256
MLA_KV_LORA = 128
MLA_NOPE = 64
MLA_ROPE = 32
MLA_V = 64
MLA_SCALE = (MLA_NOPE + MLA_ROPE) ** -0.5
ROPE_THETA = 10000.0
PEER_HEADS = 8
PEER_N_KEYS = 128
PEER_N_EXPERTS = PEER_N_KEYS * PEER_N_KEYS
PEER_KEY_DIM = 128
PEER_TOPK = 16
PEER_BLOCK = 128
Q_BLOCK = 128
NORM_EPS = 1e-6
N_BRANCH = 3
IN_PROJ_SIZES = (FOX_HEADS * FOX_HEAD_DIM, FOX_HEADS * FOX_HEAD_DIM, FOX_HEADS * FOX_HEAD_DIM, FOX_HEADS,
                 MLSTM_HEADS * MLSTM_DK, MLSTM_HEADS * MLSTM_DK, MLSTM_HEADS * MLSTM_DV, MLSTM_HEADS * MLSTM_DV,
                 MLSTM_HEADS, MLSTM_HEADS,
                 MLA_Q_LORA, MLA_KV_LORA, MLA_ROPE,
                 N_BRANCH * D_MODEL)
N_IN = sum(IN_PROJ_SIZES)

kernel_name = 'fox_mlstm_mla_peer_adaln_decode_step'


def rmsnorm(x, g):
    xf = x.astype(jnp.float32)
    y = xf * lax.rsqrt(jnp.mean(xf * xf, axis=-1, keepdims=True) + NORM_EPS)
    return (y * g.astype(jnp.float32)).astype(x.dtype)


def modulate(x, g, shift, scale):
    return rmsnorm(x, g) * (1.0 + scale[:, None, :]) + shift[:, None, :]


def adaln(c, w, b):
    return (jax.nn.silu(c) @ w + b).reshape(c.shape[0], 6, D_MODEL)


def in_proj_offsets():
    offs, acc = [], 0
    for s in IN_PROJ_SIZES[:-1]:
        acc += s
        offs.append(acc)
    return offs


def rope(x, pos):
    half = x.shape[-1] // 2
    freqs = ROPE_THETA ** (-jnp.arange(half, dtype=jnp.float32) / half)
    ang = pos.astype(jnp.float32)[:, None] * freqs[None, :]
    ang = ang.reshape((ang.shape[0],) + (1,) * (x.ndim - 3) + (half,))
    cos, sin = jnp.cos(ang), jnp.sin(ang)
    xf = x.astype(jnp.float32)
    x1, x2 = xf[..., :half], xf[..., half:]
    return jnp.concatenate([x1 * cos - x2 * sin, x1 * sin + x2 * cos], axis=-1).astype(x.dtype)


def causal_mask(q_pos, k_pos):
    return k_pos[None, :] <= q_pos[:, None]


def sweep_query_blocks(block_fn, *q_side):
    B, S = q_side[0].shape[:2]
    nb = S // Q_BLOCK
    blocks = tuple(jnp.moveaxis(a.reshape((B, nb, Q_BLOCK) + a.shape[2:]), 1, 0) for a in q_side)
    out = lax.map(lambda xs: block_fn(xs[0] * Q_BLOCK, *xs[1:]), (jnp.arange(nb),) + blocks)
    out = jnp.moveaxis(out, 0, 1)
    return out.reshape((B, S) + out.shape[3:])


def fox_prompt(q, k, v, logf):
    B, S = q.shape[:2]
    F = jnp.cumsum(logf, axis=1)
    F_k = jnp.swapaxes(F, 1, 2)
    k_pos = jnp.arange(S)
    scale = FOX_HEAD_DIM ** -0.5

    def block(start, qb, Fb):
        s = jnp.einsum('bqhd,bkhd->bhqk', qb, k).astype(jnp.float32) * scale
        s = s + jnp.swapaxes(Fb, 1, 2)[..., None] - F_k[:, :, None, :]
        mask = causal_mask(start + jnp.arange(Q_BLOCK), k_pos)
        p = jax.nn.softmax(jnp.where(mask, s, -jnp.inf), axis=-1)
        return jnp.einsum('bhqk,bkhd->bqhd', p.astype(v.dtype), v)

    o = sweep_query_blocks(block, q, F)
    return o.reshape(B, S, -1)


def fox_sample(q, k, v, logf, pool_k, pool_v, pool_logf, page_table):
    DB, T = q.shape[:2]
    scale = FOX_HEAD_DIM ** -0.5
    new_mask = causal_mask(jnp.arange(T), jnp.arange(T))

    def one(args):
        qs, ks, vs, lfs, pt = args
        kp = pool_k[pt].reshape((-1,) + pool_k.shape[2:])
        vp = pool_v[pt].reshape((-1,) + pool_v.shape[2:])
        lfp = pool_logf[pt].reshape((-1, pool_logf.shape[-1])).astype(jnp.float32)
        Fp = jnp.cumsum(lfp, axis=0)
        Fn = Fp[-1] + jnp.cumsum(lfs, axis=0)
        s_p = jnp.einsum('qhd,khd->hqk', qs, kp).astype(jnp.float32) * scale + Fn.T[:, :, None] - Fp.T[:, None, :]
        s_n = jnp.einsum('qhd,khd->hqk', qs, ks).astype(jnp.float32) * scale + Fn.T[:, :, None] - Fn.T[:, None, :]
        s_n = jnp.where(new_mask, s_n, -jnp.inf)
        p = jax.nn.softmax(jnp.concatenate([s_p, s_n], axis=-1), axis=-1).astype(vs.dtype)
        P = kp.shape[0]
        return jnp.einsum('hqk,khd->qhd', p[..., :P], vp) + jnp.einsum('hqk,khd->qhd', p[..., P:], vs)

    o = lax.map(one, (q, k, v, logf, page_table))
    return o.reshape(DB, T, -1)


def mlstm_chunk(carry, inp):
    C, n, m = (a.astype(jnp.float32) for a in carry)
    q, k, v, ig, lf = (a.astype(jnp.float32) for a in inp)
    L = q.shape[2]
    b = jnp.cumsum(lf, axis=-1)
    d = b[..., :, None] - b[..., None, :] + ig[..., None, :]
    d = jnp.where(jnp.tril(jnp.ones((L, L), bool)), d, -jnp.inf)
    inter = b + m[..., None]
    m_t = jnp.maximum(inter, jnp.max(d, axis=-1))
    w = jnp.exp(d - m_t[..., None])
    a = jnp.exp(inter - m_t)
    qk = jnp.einsum('bhtd,bhsd->bhts', q, k) * w
    num = a[..., None] * jnp.einsum('bhtd,bhde->bhte', q, C) + jnp.einsum('bhts,bhse->bhte', qk, v)
    den = a * jnp.einsum('bhtd,bhd->bht', q, n) + jnp.sum(qk, axis=-1)
    h = num / jnp.maximum(jnp.abs(den), jnp.exp(-m_t))[..., None]
    m_new = m_t[..., -1]
    a_end = jnp.exp(b[..., -1] + m - m_new)
    wk = jnp.exp(b[..., -1:] - b + ig - m_new[..., None])[..., None] * k
    C_new = a_end[..., None, None] * C + jnp.einsum('bhsd,bhse->bhde', wk, v)
    n_new = a_end[..., None] * n + jnp.sum(wk, axis=2)
    return (C_new, n_new, m_new), h


def mlstm_prompt(q, k, v, ig, lf):
    B, H, S = q.shape[:3]
    nc = S // MLSTM_CHUNK

    def chunks(a):
        return jnp.moveaxis(a.reshape((B, H, nc, MLSTM_CHUNK) + a.shape[3:]), 2, 0)

    init = (jnp.zeros((B, H, MLSTM_DK, MLSTM_DV), jnp.float32),
            jnp.zeros((B, H, MLSTM_DK), jnp.float32),
            jnp.zeros((B, H), jnp.float32))
    state, h = lax.scan(mlstm_chunk, init, tuple(chunks(a) for a in (q, k, v, ig, lf)))
    h = jnp.moveaxis(h, 0, 2).reshape(B, H, S, MLSTM_DV)
    return h, state


def mlstm_sample(q, k, v, ig, lf, C, n, m):
    state, h = mlstm_chunk((C, n, m), (q, k, v, ig, lf))
    return h, state


def mla_prompt(q_lat, q_rope, ckv, krope, w_uv):
    B, S = q_lat.shape[:2]
    k_pos = jnp.arange(S)

    def block(start, qlb, qrb):
        s = (jnp.einsum('bqhc,bkc->bhqk', qlb, ckv) + jnp.einsum('bqhr,bkr->bhqk', qrb, krope)).astype(jnp.float32) * MLA_SCALE
        mask = causal_mask(start + jnp.arange(Q_BLOCK), k_pos)
        p = jax.nn.softmax(jnp.where(mask, s, -jnp.inf), axis=-1)
        return jnp.einsum('bhqk,bkc->bqhc', p.astype(ckv.dtype), ckv)

    o_lat = sweep_query_blocks(block, q_lat, q_rope)
    return jnp.einsum('bthc,chv->bthv', o_lat, w_uv).reshape(B, S, -1)


def mla_sample(q_lat, q_rope, ckv, krope, w_uv, pool_ckv, pool_kr, page_table):
    DB, T = q_lat.shape[:2]
    new_mask = causal_mask(jnp.arange(T), jnp.arange(T))

    def one(args):
        ql, qr, cn, kn, pt = args
        cp = pool_ckv[pt].reshape(-1, pool_ckv.shape[-1])
        kp = pool_kr[pt].reshape(-1, pool_kr.shape[-1])
        s_p = (jnp.einsum('qhc,kc->hqk', ql, cp) + jnp.einsum('qhr,kr->hqk', qr, kp)).astype(jnp.float32) * MLA_SCALE
        s_n = (jnp.einsum('qhc,kc->hqk', ql, cn) + jnp.einsum('qhr,kr->hqk', qr, kn)).astype(jnp.float32) * MLA_SCALE
        s_n = jnp.where(new_mask, s_n, -jnp.inf)
        p = jax.nn.softmax(jnp.concatenate([s_p, s_n], axis=-1), axis=-1).astype(cn.dtype)
        P = cp.shape[0]
        return jnp.einsum('hqk,kc->qhc', p[..., :P], cp) + jnp.einsum('hqk,kc->qhc', p[..., P:], cn)

    o_lat = lax.map(one, (q_lat, q_rope, ckv, krope, page_table))
    return jnp.einsum('bthc,chv->bthv', o_lat, w_uv).reshape(DB, T, -1)


def project_mixers(xn, pos, lw):
    B, T, _ = xn.shape
    z = xn @ lw['w_in']
    fq, fk, fv, ff, mq, mk, mv, mo, mi, mf, cq, ckv, ckr, zg = jnp.split(z, in_proj_offsets(), axis=-1)

    def heads(a, h):
        return a.reshape(B, T, h, -1)

    fox = (heads(fq, FOX_HEADS), heads(fk, FOX_HEADS), heads(fv, FOX_HEADS),
           jax.nn.log_sigmoid((ff + lw['b_fox_f']).astype(jnp.float32)))

    def to_bh(a):
        return jnp.swapaxes(heads(a, MLSTM_HEADS), 1, 2)

    mlstm = (to_bh(mq), to_bh(mk) * (MLSTM_DK ** -0.5), to_bh(mv),
             jnp.swapaxes((mi + lw['b_mlstm_i']).astype(jnp.float32), 1, 2),
             jnp.swapaxes(jax.nn.log_sigmoid((mf + lw['b_mlstm_f']).astype(jnp.float32)), 1, 2))
    o_gate = jax.nn.sigmoid(mo)
    qf = jnp.einsum('btc,chd->bthd', rmsnorm(cq, lw['g_mla_q']), lw['w_mla_uq'])
    q_lat = jnp.einsum('bthn,chn->bthc', qf[..., :MLA_NOPE], lw['w_mla_uk'])
    q_rope = rope(qf[..., MLA_NOPE:], pos)
    mla = (q_lat, q_rope, rmsnorm(ckv, lw['g_mla_kv']), rope(ckr, pos))
    gates = jax.nn.sigmoid(zg).reshape(B, T, N_BRANCH, D_MODEL)
    return fox, mlstm, o_gate, mla, gates


def mixer_block(x, shift, scale, gate, pos, lw, fox_core, mlstm_core, mla_core):
    B, T, _ = x.shape
    xn = modulate(x, lw['g_norm_mix'], shift, scale)
    fox, mlstm, o_gate, mla, gates = project_mixers(xn, pos, lw)
    o_fox = fox_core(*fox)
    h, ml_state = mlstm_core(*mlstm)
    o_ml = o_gate * jnp.swapaxes(h, 1, 2).reshape(B, T, -1).astype(x.dtype)
    o_mla = mla_core(*mla)
    merged = (gates[:, :, 0] * (o_fox @ lw['w_branch_fox'])
              + gates[:, :, 1] * (o_ml @ lw['w_branch_mlstm'])
              + gates[:, :, 2] * (o_mla @ lw['w_branch_mla']))
    x = x + gate[:, None, :] * (merged @ lw['w_out'])
    new_rows = (fox[1], fox[2], fox[3], mla[2], mla[3]) + tuple(ml_state)
    return x, new_rows


def peer_tokens(xt, w_q, subkeys, u, v):
    N = xt.shape[0]
    q = (xt @ w_q).reshape(N, PEER_HEADS, 2, PEER_KEY_DIM // 2)
    s = jnp.einsum('nhpd,hpkd->nhpk', q, subkeys).astype(jnp.float32)
    top_s, top_i = lax.top_k(s, PEER_TOPK)
    s1, s2 = top_s[:, :, 0], top_s[:, :, 1]
    i1, i2 = top_i[:, :, 0], top_i[:, :, 1]
    cand_s = (s1[..., :, None] + s2[..., None, :]).reshape(N, PEER_HEADS, PEER_TOPK * PEER_TOPK)
    cand_i = (i1[..., :, None] * PEER_N_KEYS + i2[..., None, :]).reshape(N, PEER_HEADS, PEER_TOPK * PEER_TOPK)
    best_s, best_j = lax.top_k(cand_s, PEER_TOPK)
    idx = jnp.take_along_axis(cand_i, best_j, axis=-1)
    g = jax.nn.softmax(best_s, axis=-1)
    act = jax.nn.gelu(jnp.einsum('nhkd,nd->nhk', u[idx], xt)).astype(jnp.float32)
    coef = (g * act).astype(xt.dtype)
    return jnp.einsum('nhk,nhkd->nd', coef, v[idx])


def peer_prompt(xn, lw):
    B, S, D = xn.shape
    xb = xn.reshape(-1, PEER_BLOCK, D)
    y = lax.map(lambda t: peer_tokens(t, lw['w_peer_q'], lw['peer_subkeys'], lw['peer_u'], lw['peer_v']), xb)
    return y.reshape(B, S, D)


def peer_sample(xn, lw):
    return lax.map(lambda t: peer_tokens(t, lw['w_peer_q'], lw['peer_subkeys'], lw['peer_u'], lw['peer_v']), xn)


def ffn_block(x, shift, scale, gate, lw, peer_apply):
    xn = modulate(x, lw['g_norm_ffn'], shift, scale)
    return x + gate[:, None, :] * peer_apply(xn, lw)


def setup_inputs(seed: int = 0) -> dict:
    key = jax.random.key(seed)
    ks = iter(jax.random.split(key, 48))

    def nrm(shape, std):
        return std * jax.random.normal(next(ks), shape, jnp.float32)

    n_pages = PAST_LEN // PAGE_SIZE
    n_pool = (DEC_BATCH * n_pages * 5) // 4
    L, D = DEPTH, D_MODEL
    perm = jax.random.permutation(next(ks), n_pool)
    page_table = perm[: DEC_BATCH * n_pages].reshape(DEC_BATCH, n_pages).astype(jnp.int32)
    out = {}
    out['x_prompt'] = nrm((BATCH, SEQ, D), 1.0)
    out['x_sample'] = nrm((DEC_BATCH, DEC_SEQ, D), 1.0)
    out['c_prompt'] = nrm((BATCH, D), 1.0)
    out['c_sample'] = nrm((DEC_BATCH, D), 1.0)
    out['cache_fox_k'] = nrm((L, n_pool, PAGE_SIZE, FOX_HEADS, FOX_HEAD_DIM), 1.0)
    out['cache_fox_v'] = nrm((L, n_pool, PAGE_SIZE, FOX_HEADS, FOX_HEAD_DIM), 1.0)
    out['cache_fox_logf'] = jax.nn.log_sigmoid(FOX_FORGET_BIAS + nrm((L, n_pool, PAGE_SIZE, FOX_HEADS), 1.0))
    out['cache_mla_ckv'] = nrm((L, n_pool, PAGE_SIZE, MLA_KV_LORA), 1.0)
    out['cache_mla_krope'] = nrm((L, n_pool, PAGE_SIZE, MLA_ROPE), 1.0)
    out['state_mlstm_C'] = nrm((L, DEC_BATCH, MLSTM_HEADS, MLSTM_DK, MLSTM_DV), 0.1)
    out['state_mlstm_n'] = nrm((L, DEC_BATCH, MLSTM_HEADS, MLSTM_DK), 0.1)
    out['state_mlstm_m'] = nrm((L, DEC_BATCH, MLSTM_HEADS), 0.5)
    out['page_table'] = page_table
    out['w_ada'] = nrm((L, D, 6 * D), 0.5 * D ** -0.5)
    out['b_ada'] = nrm((L, 6 * D), 0.02)
    out['g_norm_mix'] = 1.0 + nrm((L, D), 0.05)
    out['g_norm_ffn'] = 1.0 + nrm((L, D), 0.05)
    out['w_in'] = nrm((L, D, N_IN), D ** -0.5)
    out['b_fox_f'] = FOX_FORGET_BIAS + nrm((L, FOX_HEADS), 0.5)
    out['b_mlstm_i'] = MLSTM_INPUT_BIAS + nrm((L, MLSTM_HEADS), 0.5)
    out['b_mlstm_f'] = MLSTM_FORGET_BIAS + nrm((L, MLSTM_HEADS), 0.5)
    out['g_mla_q'] = 1.0 + nrm((L, MLA_Q_LORA), 0.05)
    out['g_mla_kv'] = 1.0 + nrm((L, MLA_KV_LORA), 0.05)
    out['w_mla_uq'] = nrm((L, MLA_Q_LORA, MLA_HEADS, MLA_NOPE + MLA_ROPE), MLA_Q_LORA ** -0.5)
    out['w_mla_uk'] = nrm((L, MLA_KV_LORA, MLA_HEADS, MLA_NOPE), MLA_KV_LORA ** -0.5)
    out['w_mla_uv'] = nrm((L, MLA_KV_LORA, MLA_HEADS, MLA_V), MLA_KV_LORA ** -0.5)
    out['w_branch_fox'] = nrm((L, FOX_HEADS * FOX_HEAD_DIM, D), (FOX_HEADS * FOX_HEAD_DIM) ** -0.5)
    out['w_branch_mlstm'] = nrm((L, MLSTM_HEADS * MLSTM_DV, D), (MLSTM_HEADS * MLSTM_DV) ** -0.5)
    out['w_branch_mla'] = nrm((L, MLA_HEADS * MLA_V, D), (MLA_HEADS * MLA_V) ** -0.5)
    out['w_out'] = nrm((L, D, D), D ** -0.5)
    out['w_peer_q'] = nrm((L, D, PEER_HEADS * PEER_KEY_DIM), D ** -0.5)
    out['peer_subkeys'] = nrm((L, PEER_HEADS, 2, PEER_N_KEYS, PEER_KEY_DIM // 2), (PEER_KEY_DIM // 2) ** -0.5)
    out['peer_u'] = nrm((L, PEER_N_EXPERTS, D), D ** -0.5)
    out['peer_v'] = nrm((L, PEER_N_EXPERTS, D), PEER_HEADS ** -0.5)
    out['g_final'] = 1.0 + nrm((D,), 0.05)
    return out


def reference(x_prompt, x_sample, c_prompt, c_sample,
              cache_fox_k, cache_fox_v, cache_fox_logf, cache_mla_ckv, cache_mla_krope,
              state_mlstm_C, state_mlstm_n, state_mlstm_m, page_table,
              w_ada, b_ada, g_norm_mix, g_norm_ffn, w_in, b_fox_f, b_mlstm_i, b_mlstm_f,
              g_mla_q, g_mla_kv, w_mla_uq, w_mla_uk, w_mla_uv,
              w_branch_fox, w_branch_mlstm, w_branch_mla, w_out,
              w_peer_q, peer_subkeys, peer_u, peer_v, g_final):
    S = x_prompt.shape[1]
    T = x_sample.shape[1]
    past = page_table.shape[1] * cache_fox_k.shape[2]
    pos_p = jnp.arange(S)
    pos_s = past + jnp.arange(T)
    xp, xs = x_prompt, x_sample
    prompt_rows, sample_rows = [], []
    for l in range(DEPTH):
        lw = dict(w_in=w_in[l], b_fox_f=b_fox_f[l], b_mlstm_i=b_mlstm_i[l], b_mlstm_f=b_mlstm_f[l],
                  g_mla_q=g_mla_q[l], g_mla_kv=g_mla_kv[l], w_mla_uq=w_mla_uq[l], w_mla_uk=w_mla_uk[l],
                  w_mla_uv=w_mla_uv[l], w_branch_fox=w_branch_fox[l], w_branch_mlstm=w_branch_mlstm[l],
                  w_branch_mla=w_branch_mla[l], w_out=w_out[l], g_norm_mix=g_norm_mix[l],
                  g_norm_ffn=g_norm_ffn[l], w_peer_q=w_peer_q[l], peer_subkeys=peer_subkeys[l],
                  peer_u=peer_u[l], peer_v=peer_v[l])
        mp = adaln(c_prompt, w_ada[l], b_ada[l])
        ms = adaln(c_sample, w_ada[l], b_ada[l])
        xp, rows_p = mixer_block(xp, mp[:, 0], mp[:, 1], mp[:, 2], pos_p, lw,
                                 fox_prompt, mlstm_prompt,
                                 functools.partial(mla_prompt, w_uv=lw['w_mla_uv']))
        xp = ffn_block(xp, mp[:, 3], mp[:, 4], mp[:, 5], lw, peer_prompt)
        xs, rows_s = mixer_block(xs, ms[:, 0], ms[:, 1], ms[:, 2], pos_s, lw,
                                 functools.partial(fox_sample, pool_k=cache_fox_k[l], pool_v=cache_fox_v[l],
                                                   pool_logf=cache_fox_logf[l], page_table=page_table),
                                 functools.partial(mlstm_sample, C=state_mlstm_C[l], n=state_mlstm_n[l],
                                                   m=state_mlstm_m[l]),
                                 functools.partial(mla_sample, w_uv=lw['w_mla_uv'], pool_ckv=cache_mla_ckv[l],
                                                   pool_kr=cache_mla_krope[l], page_table=page_table))
        xs = ffn_block(xs, ms[:, 3], ms[:, 4], ms[:, 5], lw, peer_sample)
        prompt_rows.append(rows_p)
        sample_rows.append(rows_s)
    y_prompt = rmsnorm(xp, g_final)
    y_sample = rmsnorm(xs, g_final)
    (fox_k_p, fox_v_p, fox_logf_p, mla_ckv_p, mla_krope_p,
     mlstm_C_p, mlstm_n_p, mlstm_m_p) = [jnp.stack(r) for r in zip(*prompt_rows)]
    (fox_k_s, fox_v_s, fox_logf_s, mla_ckv_s, mla_krope_s,
     mlstm_C_s, mlstm_n_s, mlstm_m_s) = [jnp.stack(r) for r in zip(*sample_rows)]
    return (y_prompt, y_sample,
            fox_k_p, fox_v_p, fox_logf_p, mla_ckv_p, mla_krope_p, mlstm_C_p, mlstm_n_p, mlstm_m_p,
            fox_k_s, fox_v_s, fox_logf_s, mla_ckv_s, mla_krope_s, mlstm_C_s, mlstm_n_s, mlstm_m_s)
```

```python
import functools
import math

import jax
import jax.numpy as jnp
from jax import lax
from jax.experimental import pallas as pl
from jax.experimental.pallas import tpu as pltpu

D_MODEL = 1024
FOX_HEADS = 4
FOX_HEAD_DIM = 64
MLSTM_HEADS = 4
MLSTM_DK = 128
MLSTM_DV = 128
MLSTM_CHUNK = 128
MLA_HEADS = 4
MLA_Q_LORA = 256
MLA_KV_LORA = 128
MLA_NOPE = 64
MLA_ROPE = 32
MLA_V = 64
MLA_SCALE = (MLA_NOPE + MLA_ROPE) ** -0.5
ROPE_THETA = 10000.0
PEER_HEADS = 8
PEER_N_KEYS = 128
PEER_KEY_DIM = 128
PEER_TOPK = 16
NORM_EPS = 1e-6
N_BRANCH = 3

LANES = 128
NEG_BIG = -1e30

ZG_OFF = 0
MQ_OFF = 3072
MK_OFF = MQ_OFF + 512
MV_OFF = MK_OFF + 512
MO_OFF = MV_OFF + 512
FQ_OFF = 5120
FK_OFF = FQ_OFF + 256
FV_OFF = FK_OFF + 256
CQ_OFF = 5888
CKV_OFF = 6144
CKR_OFF = 6272
SM_OFF = 6400
Z_COLS = 6656
SM_FOXF = 0
SM_MI = 4
SM_MF = 8

_VMEM_LIMIT = 48 * 1024 * 1024
_PEER_VMEM_LIMIT = 44 * 1024 * 1024


def _cparams(sem):
    return pltpu.CompilerParams(dimension_semantics=sem, vmem_limit_bytes=_VMEM_LIMIT)


def _bf(x):
    return x.astype(jnp.bfloat16)


def _dot(a, b):
    return jnp.dot(a, b, preferred_element_type=jnp.float32)


def _dot_nt(a, b):
    return lax.dot_general(a, b, (((1,), (1,)), ((), ())), preferred_element_type=jnp.float32)


def _split3(x):
    h = _bf(x)
    r = x - h.astype(jnp.float32)
    m = _bf(r)
    l = _bf(r - m.astype(jnp.float32))
    return h, m, l


def _log_sigmoid(x):
    return jnp.minimum(x, 0.0) - jnp.log1p(jnp.exp(-jnp.abs(x)))


def _ada_kernel(c_ref, w_ref, b_ref, o_ref):
    c = c_ref[...]
    a = c * (1.0 / (1.0 + jnp.exp(-c)))
    o_ref[...] = _dot(_bf(a), _bf(w_ref[...])) + b_ref[...]


def ada_ln(c, w, b):
    R, D = c.shape
    N = w.shape[1]
    tn = 1024
    return pl.pallas_call(
        _ada_kernel,
        grid=(N // tn,),
        in_specs=[pl.BlockSpec((R, D), lambda j: (0, 0)),
                  pl.BlockSpec((D, tn), lambda j: (0, j)),
                  pl.BlockSpec((1, tn), lambda j: (0, j))],
        out_specs=pl.BlockSpec((R, tn), lambda j: (0, j)),
        out_shape=jax.ShapeDtypeStruct((R, N), jnp.float32),
        compiler_params=_cparams(("arbitrary",)),
        name="ada_ln",
    )(c, w, b.reshape(1, N))


def _modulate(x, g, shift, scale):
    y = x * lax.rsqrt(jnp.mean(x * x, axis=-1, keepdims=True) + NORM_EPS)
    return (y * g) * (1.0 + scale) + shift


def _inproj_kernel(x_ref, sh_ref, sc_ref, g_ref, w_ref, z_ref, xn_sc):
    @pl.when(pl.program_id(1) == 0)
    def _():
        xn_sc[...] = _bf(_modulate(x_ref[...], g_ref[...], sh_ref[0], sc_ref[0]))

    z_ref[...] = _dot(xn_sc[...], w_ref[...])


def in_proj(x, shift, scale, g, w_bf, rows_per_group, tm):
    N, D = x.shape
    G, R, _ = shift.shape
    tn = 512
    if R == 1:
        bpg = rows_per_group // tm
        mod_spec = pl.BlockSpec((1, 1, D), lambda i, j: (i // bpg, 0, 0))
    else:
        mod_spec = pl.BlockSpec((1, tm, D), lambda i, j: (0, i, 0))
    return pl.pallas_call(
        _inproj_kernel,
        grid=(N // tm, Z_COLS // tn),
        in_specs=[pl.BlockSpec((tm, D), lambda i, j: (i, 0)),
                  mod_spec, mod_spec,
                  pl.BlockSpec((1, D), lambda i, j: (0, 0)),
                  pl.BlockSpec((D, tn), lambda i, j: (0, j))],
        out_specs=pl.BlockSpec((tm, tn), lambda i, j: (i, j)),
        out_shape=jax.ShapeDtypeStruct((N, Z_COLS), jnp.float32),
        scratch_shapes=[pltpu.VMEM((tm, D), jnp.bfloat16)],
        compiler_params=_cparams(("arbitrary", "arbitrary")),
        name="in_proj",
    )(x, shift, scale, g.reshape(1, D), w_bf)


def _small_kernel(z_ref, bias_ref, sm_ref, cum_ref, carry_sc, *, tb):
    @pl.when(pl.program_id(1) == 0)
    def _():
        carry_sc[...] = jnp.zeros_like(carry_sc)

    v = z_ref[...] + bias_ref[...]
    lane = lax.broadcasted_iota(jnp.int32, v.shape, 1)
    is_ls = (lane < SM_MI) | ((lane >= SM_MF) & (lane < SM_MF + MLSTM_HEADS))
    sm = jnp.where(is_ls, _log_sigmoid(v), v)
    sm_ref[...] = sm
    row = lax.broadcasted_iota(jnp.int32, (tb, tb), 0)
    col = lax.broadcasted_iota(jnp.int32, (tb, tb), 1)
    tri = _bf(jnp.where(row >= col, 1.0, 0.0))
    h, m, l = _split3(sm)
    cum = _dot(tri, h) + _dot(tri, m) + _dot(tri, l) + carry_sc[...]
    cum_ref[...] = cum
    carry_sc[...] = cum[tb - 1:tb, :]


def small_gates(z, bias_row, n_groups, rows_per_group, tb):
    N = z.shape[0]
    nb = rows_per_group // tb
    cb = SM_OFF // LANES
    return pl.pallas_call(
        functools.partial(_small_kernel, tb=tb),
        grid=(n_groups, nb),
        in_specs=[pl.BlockSpec((tb, LANES), lambda b, i: (b * nb + i, cb)),
                  pl.BlockSpec((1, LANES), lambda b, i: (0, 0))],
        out_specs=[pl.BlockSpec((tb, LANES), lambda b, i: (b * nb + i, 0)),
                   pl.BlockSpec((tb, LANES), lambda b, i: (b * nb + i, 0))],
        out_shape=[jax.ShapeDtypeStruct((N, LANES), jnp.float32),
                   jax.ShapeDtypeStruct((N, LANES), jnp.float32)],
        scratch_shapes=[pltpu.VMEM((1, LANES), jnp.float32)],
        compiler_params=_cparams(("arbitrary", "arbitrary")),
        name="small_gates",
    )(z, bias_row)


def _fox_prompt_kernel(q_ref, k_ref, v_ref, fq_ref, fk_ref, o_ref, m_sc, l_sc, acc_sc, *, t):
    qi = pl.program_id(1)
    ki = pl.program_id(2)

    @pl.when(ki == 0)
    def _():
        m_sc[...] = jnp.full_like(m_sc, NEG_BIG)
        l_sc[...] = jnp.zeros_like(l_sc)
        acc_sc[...] = jnp.zeros_like(acc_sc)

    @pl.when(ki <= qi)
    def _():
        scale = FOX_HEAD_DIM ** -0.5
        row = lax.broadcasted_iota(jnp.int32, (t, t), 0)
        col = lax.broadcasted_iota(jnp.int32, (t, t), 1)
        keep = (qi * t + row) >= (ki * t + col)
        fq = fq_ref[...]
        fk = fk_ref[0]
        half = lax.broadcasted_iota(jnp.int32, (1, LANES), 1) // FOX_HEAD_DIM
        for h in range(FOX_HEADS):
            sl = slice((h // 2) * LANES, (h // 2 + 1) * LANES)
            mine = half == (h % 2)
            qm = jnp.where(mine, q_ref[:, sl], 0.0)
            s = _dot_nt(_bf(qm), _bf(k_ref[:, sl])) * scale
            s = s + fq[:, h:h + 1] - fk[h:h + 1, :]
            s = jnp.where(keep, s, NEG_BIG)
            m_prev = m_sc[h]
            m_new = jnp.maximum(m_prev, jnp.max(s, axis=-1, keepdims=True))
            p = jnp.exp(s - m_new)
            alpha = jnp.exp(m_prev - m_new)
            l_sc[h] = alpha * l_sc[h] + jnp.sum(p, axis=-1, keepdims=True)
            pv = _dot(_bf(p), _bf(v_ref[:, sl]))
            acc = acc_sc[h // 2]
            acc_sc[h // 2] = jnp.where(mine, alpha * acc + pv, acc)
            m_sc[h] = m_new

    @pl.when(ki == qi)
    def _():
        half = lax.broadcasted_iota(jnp.int32, (1, LANES), 1) // FOX_HEAD_DIM
        for pr in range(FOX_HEADS // 2):
            inv = jnp.where(half == 0, 1.0 / l_sc[2 * pr], 1.0 / l_sc[2 * pr + 1])
            o_ref[:, pr * LANES:(pr + 1) * LANES] = acc_sc[pr] * inv


def fox_prompt(z, cum, cum_t, B, S, t):
    nq = S // t
    hd = FOX_HEADS * FOX_HEAD_DIM
    qb, kb, vb = FQ_OFF // hd, FK_OFF // hd, FV_OFF // hd
    return pl.pallas_call(
        functools.partial(_fox_prompt_kernel, t=t),
        grid=(B, nq, nq),
        in_specs=[pl.BlockSpec((t, hd), lambda b, i, j: (b * nq + i, qb)),
                  pl.BlockSpec((t, hd), lambda b, i, j: (b * nq + jnp.minimum(i, j), kb)),
                  pl.BlockSpec((t, hd), lambda b, i, j: (b * nq + jnp.minimum(i, j), vb)),
                  pl.BlockSpec((t, LANES), lambda b, i, j: (b * nq + i, 0)),
                  pl.BlockSpec((1, 8, t), lambda b, i, j: (b, 0, jnp.minimum(i, j)))],
        out_specs=pl.BlockSpec((t, hd), lambda b, i, j: (b * nq + i, 0)),
        out_shape=jax.ShapeDtypeStruct((B * S, hd), jnp.float32),
        scratch_shapes=[pltpu.VMEM((FOX_HEADS, t, 1), jnp.float32),
                        pltpu.VMEM((FOX_HEADS, t, 1), jnp.float32),
                        pltpu.VMEM((FOX_HEADS // 2, t, LANES), jnp.float32)],
        compiler_params=_cparams(("arbitrary", "arbitrary", "arbitrary")),
        name="fox_prompt",
    )(z, z, z, cum, cum_t)


def _rms(x, g):
    return x * lax.rsqrt(jnp.mean(x * x, axis=-1, keepdims=True) + NORM_EPS) * g


def _rope_tiled(v, cos_t, sin_t):
    return v * cos_t + pltpu.roll(v, LANES // 2, axis=1) * sin_t


def _mla_prep_kernel(cq_ref, ckv_ref, ckr_ref, cos_ref, sin_ref, gq_ref, wuq_ref, wuk_ref, gkv_ref,
                     qc_ref, kc_ref, ckvn_ref, krt_ref, qrt_ref):
    cos_t = cos_ref[...]
    sin_t = sin_ref[...]
    cqn = _rms(cq_ref[...], gq_ref[...])
    qf = _dot(_bf(cqn), wuq_ref[...])
    nq = MLA_HEADS * MLA_NOPE
    q_lat = _dot(_bf(qf[:, :nq]), wuk_ref[...])
    q_rope = _rope_tiled(qf[:, nq:], cos_t, sin_t)
    qrt_ref[...] = q_rope
    lane = lax.broadcasted_iota(jnp.int32, (1, LANES), 1)
    owner = (lane % (LANES // 2)) // (MLA_ROPE // 2)
    for h in range(MLA_HEADS):
        qc_ref[h, :, :MLA_KV_LORA] = _bf(q_lat[:, h * MLA_KV_LORA:(h + 1) * MLA_KV_LORA])
        qc_ref[h, :, MLA_KV_LORA:] = _bf(jnp.where(owner == h, q_rope, 0.0))
    ckvn = _rms(ckv_ref[...], gkv_ref[...])
    ckvn_ref[...] = ckvn
    k_rope = _rope_tiled(ckr_ref[...], cos_t, sin_t)
    krt_ref[...] = k_rope
    kc_ref[:, :MLA_KV_LORA] = _bf(ckvn)
    kc_ref[:, MLA_KV_LORA:] = _bf(k_rope)


def mla_prep(z, cos_t, sin_t, table_blocks, g_q, w_uq_r, w_uk_bd, g_kv, tm):
    N = z.shape[0]
    cw = 2 * MLA_KV_LORA
    tab = pl.BlockSpec((tm, LANES), lambda i: (i % table_blocks, 0))
    full = lambda a: pl.BlockSpec(a.shape, lambda i: (0,) * a.ndim)
    gq = g_q.reshape(1, -1)
    gkv = g_kv.reshape(1, -1)
    return pl.pallas_call(
        _mla_prep_kernel,
        grid=(N // tm,),
        in_specs=[pl.BlockSpec((tm, MLA_Q_LORA), lambda i: (i, CQ_OFF // MLA_Q_LORA)),
                  pl.BlockSpec((tm, LANES), lambda i: (i, CKV_OFF // LANES)),
                  pl.BlockSpec((tm, LANES), lambda i: (i, CKR_OFF // LANES)),
                  tab, tab, full(gq), full(w_uq_r), full(w_uk_bd), full(gkv)],
        out_specs=[pl.BlockSpec((MLA_HEADS, tm, cw), lambda i: (0, i, 0)),
                   pl.BlockSpec((tm, cw), lambda i: (i, 0)),
                   pl.BlockSpec((tm, LANES), lambda i: (i, 0)),
                   pl.BlockSpec((tm, LANES), lambda i: (i, 0)),
                   pl.BlockSpec((tm, LANES), lambda i: (i, 0))],
        out_shape=[jax.ShapeDtypeStruct((MLA_HEADS, N, cw), jnp.bfloat16),
                   jax.ShapeDtypeStruct((N, cw), jnp.bfloat16),
                   jax.ShapeDtypeStruct((N, LANES), jnp.float32),
                   jax.ShapeDtypeStruct((N, LANES), jnp.float32),
                   jax.ShapeDtypeStruct((N, LANES), jnp.float32)],
        compiler_params=_cparams(("arbitrary",)),
        name="mla_prep",
    )(z, z, z, cos_t, sin_t, gq, w_uq_r, w_uk_bd, gkv)


def _mla_prompt_kernel(q_ref, k_ref, o_ref, m_sc, l_sc, acc_sc, *, t):
    qi = pl.program_id(1)
    ki = pl.program_id(2)
    rows = MLA_HEADS * t

    @pl.when(ki == 0)
    def _():
        m_sc[...] = jnp.full_like(m_sc, NEG_BIG)
        l_sc[...] = jnp.zeros_like(l_sc)
        acc_sc[...] = jnp.zeros_like(acc_sc)

    @pl.when(ki <= qi)
    def _():
        q = q_ref[...].reshape(rows, 2 * MLA_KV_LORA)
        k = k_ref[...]
        s = _dot_nt(q, k) * MLA_SCALE
        row = lax.broadcasted_iota(jnp.int32, (rows, t), 0) & (t - 1)
        col = lax.broadcasted_iota(jnp.int32, (rows, t), 1)
        s = jnp.where((qi * t + row) >= (ki * t + col), s, NEG_BIG)
        m_prev = m_sc[...]
        m_new = jnp.maximum(m_prev, jnp.max(s, axis=-1, keepdims=True))
        p = jnp.exp(s - m_new)
        alpha = jnp.exp(m_prev - m_new)
        l_sc[...] = alpha * l_sc[...] + jnp.sum(p, axis=-1, keepdims=True)
        acc_sc[...] = alpha * acc_sc[...] + _dot(_bf(p), k[:, :MLA_KV_LORA])
        m_sc[...] = m_new

    @pl.when(ki == qi)
    def _():
        o_ref[...] = (acc_sc[...] / l_sc[...]).reshape(MLA_HEADS, t, MLA_KV_LORA)


def mla_prompt(qc, kc, B, S, t):
    nq = S // t
    cw = 2 * MLA_KV_LORA
    rows = MLA_HEADS * t
    return pl.pallas_call(
        functools.partial(_mla_prompt_kernel, t=t),
        grid=(B, nq, nq),
        in_specs=[pl.BlockSpec((MLA_HEADS, t, cw), lambda b, i, j: (0, b * nq + i, 0)),
                  pl.BlockSpec((t, cw), lambda b, i, j: (b * nq + jnp.minimum(i, j), 0))],
        out_specs=pl.BlockSpec((MLA_HEADS, t, MLA_KV_LORA), lambda b, i, j: (0, b * nq + i, 0)),
        out_shape=jax.ShapeDtypeStruct((MLA_HEADS, B * S, MLA_KV_LORA), jnp.float32),
        scratch_shapes=[pltpu.VMEM((rows, 1), jnp.float32),
                        pltpu.VMEM((rows, 1), jnp.float32),
                        pltpu.VMEM((rows, MLA_KV_LORA), jnp.float32)],
        compiler_params=_cparams(("arbitrary", "arbitrary", "arbitrary")),
        name="mla_prompt",
    )(qc, kc)


def _mlstm_kernel(*refs, L, l_real, zero_init):
    if zero_init:
        q_ref, k_ref, v_ref, kt_ref, sm_ref, smt_ref, h_ref, c_ref, n_ref, m_ref, c_sc, n_sc, m_sc = refs
    else:
        (q_ref, k_ref, v_ref, kt_ref, sm_ref, smt_ref, c0_ref, n0_ref, m0_ref,
         h_ref, c_ref, n_ref, m_ref, c_sc, n_sc, m_sc) = refs
    ci = pl.program_id(1)

    @pl.when(ci == 0)
    def _():
        if zero_init:
            c_sc[...] = jnp.zeros_like(c_sc)
            n_sc[...] = jnp.zeros_like(n_sc)
            m_sc[...] = jnp.zeros_like(m_sc)
        else:
            c_sc[...] = c0_ref[0]
            n_sc[...] = n0_ref[0]
            m_sc[...] = m0_ref[0]

    kscale = MLSTM_DK ** -0.5
    row = lax.broadcasted_iota(jnp.int32, (L, L), 0)
    col = lax.broadcasted_iota(jnp.int32, (L, L), 1)
    tri = row >= col
    real_r = lax.broadcasted_iota(jnp.int32, (1, L), 1) < l_real
    real_c = lax.broadcasted_iota(jnp.int32, (L, 1), 0) < l_real
    sm = sm_ref[...]
    smt = smt_ref[0]
    last = l_real - 1
    for h in range(MLSTM_HEADS):
        sl = slice(h * MLSTM_DK, (h + 1) * MLSTM_DK)
        qh = q_ref[:, sl]
        kh = k_ref[:, sl] * kscale
        vh = _bf(v_ref[:, sl])
        kth = kt_ref[0, h] * kscale
        ig_r = smt[SM_MI + h:SM_MI + h + 1, :]
        lf_r = smt[SM_MF + h:SM_MF + h + 1, :]
        ig_c = sm[:, SM_MI + h:SM_MI + h + 1]
        lf_c = sm[:, SM_MF + h:SM_MF + h + 1]
        b_c = jnp.sum(jnp.where(tri, lf_r, 0.0), axis=1, keepdims=True)
        b_r = jnp.sum(jnp.where(col >= row, lf_c, 0.0), axis=0, keepdims=True)
        d = jnp.where(tri, b_c - b_r + ig_r, NEG_BIG)
        m_prev = m_sc[h][:, :1]
        inter = b_c + m_prev
        m_t = jnp.maximum(inter, jnp.max(d, axis=1, keepdims=True))
        w = jnp.exp(d - m_t)
        a = jnp.exp(inter - m_t)
        qb = _bf(qh)
        qk = _dot_nt(qb, _bf(kh)) * w
        c_prev = c_sc[h]
        n_prev = n_sc[h]
        num = a * _dot(qb, _bf(c_prev)) + _dot(_bf(qk), vh)
        den = a * jnp.sum(qh * n_prev, axis=1, keepdims=True) + jnp.sum(qk, axis=1, keepdims=True)
        h_ref[:, sl] = num / jnp.maximum(jnp.abs(den), jnp.exp(-m_t))
        m_new = m_t[last:last + 1, :]
        b_last = b_c[last:last + 1, :]
        a_end = jnp.exp(b_last + m_prev - m_new)
        w_r = jnp.where(real_r, jnp.exp(b_last - b_r + ig_r - m_new), 0.0)
        w_c = jnp.where(real_c, jnp.exp(b_last - b_c + ig_c - m_new), 0.0)
        c_sc[h] = a_end * c_prev + _dot(_bf(kth * w_r), vh)
        n_sc[h] = a_end * n_prev + jnp.sum(w_c * kh, axis=0, keepdims=True)
        m_sc[h] = jnp.broadcast_to(m_new, (1, LANES))

    @pl.when(ci == pl.num_programs(1) - 1)
    def _():
        c_ref[0] = c_sc[...]
        n_ref[0] = n_sc[...]
        m_ref[0] = m_sc[...]


def mlstm_scan(q, k, v, colblocks, kt, sm, smt, state, Bm, nc, L, l_real):
    H, DK, DV = MLSTM_HEADS, MLSTM_DK, MLSTM_DV
    zero_init = state is None
    rows = Bm * nc * L
    hw = H * DK
    qs, ks_, vs = (pl.BlockSpec((L, hw), functools.partial(lambda b, c, cb: (b * nc + c, cb), cb=cb))
                   for cb in colblocks)
    in_specs = [qs, ks_, vs,
                pl.BlockSpec((1, H, DK, L), lambda b, c: (b, 0, 0, c)),
                pl.BlockSpec((L, LANES), lambda b, c: (b * nc + c, 0)),
                pl.BlockSpec((1, 16, L), lambda b, c: (b * nc + c, 0, 0))]
    args = [q, k, v, kt, sm, smt]
    st_specs = [pl.BlockSpec((1, H, DK, DV), lambda b, c: (b, 0, 0, 0)),
                pl.BlockSpec((1, H, 1, DK), lambda b, c: (b, 0, 0, 0)),
                pl.BlockSpec((1, H, 1, LANES), lambda b, c: (b, 0, 0, 0))]
    if not zero_init:
        in_specs += st_specs
        args += list(state)
    return pl.pallas_call(
        functools.partial(_mlstm_kernel, L=L, l_real=l_real, zero_init=zero_init),
        grid=(Bm, nc),
        in_specs=in_specs,
        out_specs=[pl.BlockSpec((L, hw), lambda b, c: (b * nc + c, 0))] + st_specs,
        out_shape=[jax.ShapeDtypeStruct((rows, hw), jnp.float32),
                   jax.ShapeDtypeStruct((Bm, H, DK, DV), jnp.float32),
                   jax.ShapeDtypeStruct((Bm, H, 1, DK), jnp.float32),
                   jax.ShapeDtypeStruct((Bm, H, 1, LANES), jnp.float32)],
        scratch_shapes=[pltpu.VMEM((H, DK, DV), jnp.float32),
                        pltpu.VMEM((H, 1, DK), jnp.float32),
                        pltpu.VMEM((H, 1, LANES), jnp.float32)],
        compiler_params=_cparams(("arbitrary", "arbitrary")),
        name="mlstm_scan",
    )(*args)


def _merge_kernel(x_ref, gate_ref, sh2_ref, sc2_ref, ofox_ref, hml_ref, mo_ref, zg0_ref, zg1_ref, zg2_ref,
                  olat_ref, wbf_ref, wbm_ref, wuv_ref, wbc_ref, wout_ref, g2_ref, x1_ref, xn2_ref):
    sig = lambda u: 1.0 / (1.0 + jnp.exp(-u))
    o_ml = sig(mo_ref[...]) * hml_ref[...]
    o_lat = jnp.concatenate([olat_ref[h] for h in range(MLA_HEADS)], axis=1)
    o_mla = _dot(_bf(o_lat), wuv_ref[...])
    merged = (sig(zg0_ref[...]) * _dot(_bf(ofox_ref[...]), wbf_ref[...])
              + sig(zg1_ref[...]) * _dot(_bf(o_ml), wbm_ref[...])
              + sig(zg2_ref[...]) * _dot(_bf(o_mla), wbc_ref[...]))
    x1 = x_ref[...] + gate_ref[0] * _dot(_bf(merged), wout_ref[...])
    x1_ref[...] = x1
    xn2_ref[...] = _modulate(x1, g2_ref[...], sh2_ref[0], sc2_ref[0])


def merge_block(x, gate, shift2, scale2, o_fox, h_ml, z, o_lat, wbf, wbm, wuv_bd, wbc, wout, g2,
                rows_per_group, tm):
    N, D = x.shape
    G, R, _ = gate.shape
    if R == 1:
        bpg = rows_per_group // tm
        mod_spec = pl.BlockSpec((1, 1, D), lambda i: (i // bpg, 0, 0))
    else:
        mod_spec = pl.BlockSpec((1, tm, D), lambda i: (0, i, 0))
    full = lambda a: pl.BlockSpec(a.shape, lambda i: (0,) * a.ndim)
    rowblk = lambda w, cb=0: pl.BlockSpec((tm, w), lambda i: (i, cb))
    g2r = g2.reshape(1, D)
    return pl.pallas_call(
        _merge_kernel,
        grid=(N // tm,),
        in_specs=[rowblk(D), mod_spec, mod_spec, mod_spec,
                  rowblk(256), rowblk(512), rowblk(512, MO_OFF // 512),
                  rowblk(D, 0), rowblk(D, 1), rowblk(D, 2),
                  pl.BlockSpec((MLA_HEADS, tm, MLA_KV_LORA), lambda i: (0, i, 0)),
                  full(wbf), full(wbm), full(wuv_bd), full(wbc), full(wout), full(g2r)],
        out_specs=[rowblk(D), rowblk(D)],
        out_shape=[jax.ShapeDtypeStruct((N, D), jnp.float32),
                   jax.ShapeDtypeStruct((N, D), jnp.float32)],
        compiler_params=_cparams(("arbitrary",)),
        name="merge_block",
    )(x, gate, shift2, scale2, o_fox, h_ml, z, z, z, z, o_lat, wbf, wbm, wuv_bd, wbc, wout, g2r)


FOX_PAGE_COLS = 512


def _pool_logf_kernel(x_ref, w_ref):
    n = FOX_PAGE_COLS
    r = lax.broadcasted_iota(jnp.int32, (n, 2 * n), 0)
    c = lax.broadcasted_iota(jnp.int32, (n, 2 * n), 1)
    same_head = (r % FOX_HEADS) == (c % FOX_HEADS)
    sel = same_head & ((c >= n) | (r > c))
    mat = _bf(jnp.where(sel, 1.0, 0.0))
    h, m, l = _split3(x_ref[...])
    w_ref[...] = _dot(h, mat) + _dot(m, mat) + _dot(l, mat)


def pool_logf_suffix(logf_flat, tm):
    n_rows = logf_flat.shape[0]
    n = FOX_PAGE_COLS
    assert n_rows % tm == 0 and logf_flat.shape[1] == n
    return pl.pallas_call(
        _pool_logf_kernel,
        grid=(n_rows // tm,),
        in_specs=[pl.BlockSpec((tm, n), lambda i: (i, 0))],
        out_specs=pl.BlockSpec((tm, 2 * n), lambda i: (i, 0)),
        out_shape=jax.ShapeDtypeStruct((n_rows, 2 * n), jnp.float32),
        compiler_params=_cparams(("arbitrary",)),
        name="pool_logf_suffix",
    )(logf_flat)


def _online_update(s, v_bf, m_sc, l_sc, acc_sc):
    m_prev = m_sc[...]
    m_new = jnp.maximum(m_prev, jnp.max(s, axis=-1, keepdims=True))
    p = jnp.exp(s - m_new)
    alpha = jnp.exp(m_prev - m_new)
    l_sc[...] = alpha * l_sc[...] + jnp.sum(p, axis=-1, keepdims=True)
    acc_sc[...] = alpha * acc_sc[...] + _dot(_bf(p), v_bf)
    m_sc[...] = m_new


def _fox_decode_kernel(pt_ref, q_ref, kn_ref, vn_ref, lfr_ref, lfc_ref, *rest, P, T):
    k_refs, v_refs, w_refs = rest[:P], rest[P:2 * P], rest[2 * P:3 * P]
    o_ref, m_sc, l_sc, acc_sc, carry_sc = rest[3 * P:]
    s_id = pl.program_id(1)
    R = FOX_HEADS * T
    scale = FOX_HEAD_DIM ** -0.5
    qb = _bf(q_ref[0])

    @pl.when(s_id == 0)
    def _():
        r = lax.broadcasted_iota(jnp.int32, (R, R), 0)
        c = lax.broadcasted_iota(jnp.int32, (R, R), 1)
        pre = ((r % FOX_HEADS) == (c % FOX_HEADS)) & (r <= c)
        cn = jnp.sum(jnp.where(pre, lfc_ref[0], 0.0), axis=0, keepdims=True)
        s = _dot_nt(qb, _bf(kn_ref[0])) * scale - cn
        ok = ((c % FOX_HEADS) == (r // T)) & ((c // FOX_HEADS) <= (r % T))
        s = jnp.where(ok, s, NEG_BIG)
        m = jnp.max(s, axis=-1, keepdims=True)
        p = jnp.exp(s - m)
        m_sc[...] = m
        l_sc[...] = jnp.sum(p, axis=-1, keepdims=True)
        acc_sc[...] = _dot(_bf(p), _bf(vn_ref[0]))
        carry_sc[...] = jnp.zeros_like(carry_sc)

    n = FOX_PAGE_COLS
    r = lax.broadcasted_iota(jnp.int32, (R, n), 0)
    c = lax.broadcasted_iota(jnp.int32, (R, n), 1)
    ok = (c % FOX_HEADS) == (r // T)
    for i in range(P):
        w = w_refs[i][0]
        s = _dot_nt(qb, _bf(k_refs[i][0])) * scale + (w[:, :n] + carry_sc[...])
        s = jnp.where(ok, s, NEG_BIG)
        _online_update(s, _bf(v_refs[i][0]), m_sc, l_sc, acc_sc)
        carry_sc[...] = carry_sc[...] + w[:, n:]

    @pl.when(s_id == pl.num_programs(1) - 1)
    def _():
        o_ref[0] = acc_sc[...] / l_sc[...]


def fox_decode(page_table, q, k_new, v_new, lf_row, lf_col, pool_k, pool_v, pool_w, page_off, P):
    DB, R, dh = q.shape
    n_pages = page_table.shape[1]
    T = R // FOX_HEADS
    steps = n_pages // P
    n = FOX_PAGE_COLS

    def page_map(i):
        return lambda b, s, pt: (page_off + pt[b, n_pages - 1 - (s * P + i)], 0, 0)

    per_b = lambda shape: pl.BlockSpec((1,) + shape, lambda b, s, pt: (b, 0, 0))
    in_specs = [per_b((R, dh)), per_b((R, dh)), per_b((R, dh)), per_b((1, R)), per_b((R, 1))]
    in_specs += [pl.BlockSpec((1, n, dh), page_map(i)) for i in range(P)]
    in_specs += [pl.BlockSpec((1, n, dh), page_map(i)) for i in range(P)]
    in_specs += [pl.BlockSpec((1, 1, 2 * n), page_map(i)) for i in range(P)]
    grid_spec = pltpu.PrefetchScalarGridSpec(
        num_scalar_prefetch=1, grid=(DB, steps), in_specs=in_specs,
        out_specs=per_b((R, dh)),
        scratch_shapes=[pltpu.VMEM((R, 1), jnp.float32), pltpu.VMEM((R, 1), jnp.float32),
                        pltpu.VMEM((R, dh), jnp.float32), pltpu.VMEM((1, n), jnp.float32)])
    return pl.pallas_call(
        functools.partial(_fox_decode_kernel, P=P, T=T),
        grid_spec=grid_spec,
        out_shape=jax.ShapeDtypeStruct((DB, R, dh), jnp.float32),
        compiler_params=_cparams(("arbitrary", "arbitrary")),
        name="fox_decode",
    )(page_table, q, k_new, v_new, lf_row, lf_col, *([pool_k] * P), *([pool_v] * P), *([pool_w] * P))


def _mla_decode_kernel(pt_ref, ql_ref, qr_ref, cn_ref, kn_ref, *rest, P, T):
    c_refs, k_refs = rest[:P], rest[P:2 * P]
    o_ref, m_sc, l_sc, acc_sc = rest[2 * P:]
    s_id = pl.program_id(1)
    R = MLA_HEADS * T
    ql = _bf(ql_ref[0])
    qr = _bf(qr_ref[0])

    @pl.when(s_id == 0)
    def _():
        cn = _bf(cn_ref[0])
        s = (_dot_nt(ql, cn) + _dot_nt(qr, _bf(kn_ref[0]))) * MLA_SCALE
        r = lax.broadcasted_iota(jnp.int32, (R, T), 0)
        c = lax.broadcasted_iota(jnp.int32, (R, T), 1)
        s = jnp.where(c <= (r % T), s, NEG_BIG)
        m = jnp.max(s, axis=-1, keepdims=True)
        p = jnp.exp(s - m)
        m_sc[...] = m
        l_sc[...] = jnp.sum(p, axis=-1, keepdims=True)
        acc_sc[...] = _dot(_bf(p), cn)

    for i in range(P):
        cp = _bf(c_refs[i][0])
        s = (_dot_nt(ql, cp) + _dot_nt(qr, _bf(k_refs[i][0]))) * MLA_SCALE
        _online_update(s, cp, m_sc, l_sc, acc_sc)

    @pl.when(s_id == pl.num_programs(1) - 1)
    def _():
        o_ref[0] = acc_sc[...] / l_sc[...]


def mla_decode(page_table, q_lat, q_rope, ckv_new, kr_new, pool_ckv, pool_kr, page_off, P):
    DB, R, C = q_lat.shape
    n_pages = page_table.shape[1]
    T = R // MLA_HEADS
    steps = n_pages // P
    ps, rr = pool_ckv.shape[1], pool_kr.shape[2]

    def page_map(i):
        return lambda b, s, pt: (page_off + pt[b, s * P + i], 0, 0)

    per_b = lambda shape: pl.BlockSpec((1,) + shape, lambda b, s, pt: (b, 0, 0))
    in_specs = [per_b((R, C)), per_b((R, rr)), per_b((T, C)), per_b((T, rr))]
    in_specs += [pl.BlockSpec((1, ps, C), page_map(i)) for i in range(P)]
    in_specs += [pl.BlockSpec((1, ps, rr), page_map(i)) for i in range(P)]
    grid_spec = pltpu.PrefetchScalarGridSpec(
        num_scalar_prefetch=1, grid=(DB, steps), in_specs=in_specs,
        out_specs=per_b((R, C)),
        scratch_shapes=[pltpu.VMEM((R, 1), jnp.float32), pltpu.VMEM((R, 1), jnp.float32),
                        pltpu.VMEM((R, C), jnp.float32)])
    return pl.pallas_call(
        functools.partial(_mla_decode_kernel, P=P, T=T),
        grid_spec=grid_spec,
        out_shape=jax.ShapeDtypeStruct((DB, R, C), jnp.float32),
        compiler_params=_cparams(("arbitrary", "arbitrary")),
        name="mla_decode",
    )(page_table, q_lat, q_rope, ckv_new, kr_new, *([pool_ckv] * P), *([pool_kr] * P))


def _top_rows(vals, payload, k):
    n = vals.shape[0]
    pos = lax.broadcasted_iota(jnp.int32, vals.shape, 0)
    top_v, top_p = [], []
    for _ in range(k):
        m = jnp.max(vals, axis=0, keepdims=True)
        at = jnp.min(jnp.where(vals == m, pos, n), axis=0, keepdims=True)
        hit = pos == at
        top_v.append(m)
        top_p.append(at if payload is None else jnp.sum(jnp.where(hit, payload, 0), axis=0, keepdims=True))
        vals = jnp.where(hit, -jnp.inf, vals)
    return jnp.concatenate(top_v, axis=0), jnp.concatenate(top_p, axis=0)


def _peer_select_kernel(x_ref, wh_ref, wl_ref, skh_ref, skl_ref, idx_ref, g_ref):
    x = x_ref[...]
    xh = _bf(x)
    xl = _bf(x - xh.astype(jnp.float32))
    wh = wh_ref[...]
    qt = _dot_nt(wh, xh) + _dot_nt(wh, xl) + _dot_nt(wl_ref[...], xh)
    half = PEER_KEY_DIM // 2
    K = PEER_TOPK
    for h in range(PEER_HEADS):
        tops = []
        for p in range(2):
            j = 2 * h + p
            q = qt[j * half:(j + 1) * half, :]
            qh = _bf(q)
            ql = _bf(q - qh.astype(jnp.float32))
            skh = skh_ref[j]
            s = _dot(skh, qh) + _dot(skh, ql) + _dot(skl_ref[j], qh)
            tops.append(_top_rows(s, None, K))
        (s1, i1), (s2, i2) = tops
        cand_s = jnp.concatenate([s1[a:a + 1, :] + s2 for a in range(K)], axis=0)
        cand_i = jnp.concatenate([i1[a:a + 1, :] * PEER_N_KEYS + i2 for a in range(K)], axis=0)
        best_s, best_i = _top_rows(cand_s, cand_i, K)
        e = jnp.exp(best_s - best_s[0:1, :])
        g_ref[0, h * K:(h + 1) * K, :] = e / jnp.sum(e, axis=0, keepdims=True)
        idx_ref[0, h * K:(h + 1) * K, :] = best_i


def peer_select(xn, wq_t_hi, wq_t_lo, sk_hi, sk_lo, tm):
    N, D = xn.shape
    HK = PEER_HEADS * PEER_TOPK
    full = lambda a: pl.BlockSpec(a.shape, lambda i: (0,) * a.ndim)
    return pl.pallas_call(
        _peer_select_kernel,
        grid=(N // tm,),
        in_specs=[pl.BlockSpec((tm, D), lambda i: (i, 0)),
                  full(wq_t_hi), full(wq_t_lo), full(sk_hi), full(sk_lo)],
        out_specs=[pl.BlockSpec((1, HK, tm), lambda i: (i, 0, 0)),
                   pl.BlockSpec((1, HK, tm), lambda i: (i, 0, 0))],
        out_shape=[jax.ShapeDtypeStruct((N // tm, HK, tm), jnp.int32),
                   jax.ShapeDtypeStruct((N // tm, HK, tm), jnp.float32)],
        compiler_params=_cparams(("arbitrary",)),
        name="peer_select",
    )(xn, wq_t_hi, wq_t_lo, sk_hi, sk_lo)


PEER_ROWS = PEER_HEADS * PEER_TOPK
PK_SUB = 4


def _unpack_row(w):
    hi = lax.bitcast_convert_type(w & jnp.uint32(0xFFFF0000), jnp.float32)
    lo = lax.bitcast_convert_type(w << 16, jnp.float32)
    return hi, lo


def _peer_act_kernel(idx_ref, x_ref, tab_ref, o_ref, *, tb):
    lane = lax.broadcasted_iota(jnp.int32, (8, LANES), 1)

    def token(t, carry):
        xh = x_ref[t, 0:PK_SUB, :]
        xl = x_ref[t, PK_SUB:2 * PK_SUB, :]
        tile = jnp.zeros((8, LANES), jnp.float32)
        for gi in range(PEER_ROWS // 8):
            rows = []
            for s in range(8):
                hi, lo = _unpack_row(tab_ref[idx_ref[t, gi * 8 + s]])
                rows.append(jnp.sum(hi * xh + lo * xl, axis=0, keepdims=True))
            grp = jnp.concatenate(rows, axis=0)
            tile = jnp.where(lane == gi, jnp.sum(grp, axis=1, keepdims=True), tile)
        o_ref[t] = tile
        return carry

    lax.fori_loop(0, tb, token, 0)


def peer_act(idx, x8, u_pk, tb):
    N = idx.shape[0]
    return pl.pallas_call(
        functools.partial(_peer_act_kernel, tb=tb),
        grid=(N // tb,),
        in_specs=[pl.BlockSpec((tb, PEER_ROWS), lambda i: (i, 0), memory_space=pltpu.SMEM),
                  pl.BlockSpec((tb, 8, LANES), lambda i: (i, 0, 0)),
                  pl.BlockSpec(u_pk.shape, lambda i: (0, 0, 0), pipeline_mode=pl.Buffered(1))],
        out_specs=pl.BlockSpec((tb, 8, LANES), lambda i: (i, 0, 0)),
        out_shape=jax.ShapeDtypeStruct((N, 8, LANES), jnp.float32),
        compiler_params=pltpu.CompilerParams(dimension_semantics=("arbitrary",),
                                             vmem_limit_bytes=_PEER_VMEM_LIMIT),
        name="peer_act",
    )(idx, x8, u_pk)


def _peer_mix_kernel(idx_ref, coef_ref, tab_ref, o_ref, *, tb):
    def token(t, carry):
        acc_h = jnp.zeros((PK_SUB, LANES), jnp.float32)
        acc_l = jnp.zeros((PK_SUB, LANES), jnp.float32)
        for r in range(PEER_ROWS):
            hi, lo = _unpack_row(tab_ref[idx_ref[t, r]])
            c = coef_ref[t, r]
            acc_h = acc_h + c * hi
            acc_l = acc_l + c * lo
        o_ref[t] = jnp.concatenate([acc_h, acc_l], axis=0)
        return carry

    lax.fori_loop(0, tb, token, 0)


def peer_mix(idx, coef, v_pk, tb):
    N = idx.shape[0]
    return pl.pallas_call(
        functools.partial(_peer_mix_kernel, tb=tb),
        grid=(N // tb,),
        in_specs=[pl.BlockSpec((tb, PEER_ROWS), lambda i: (i, 0), memory_space=pltpu.SMEM),
                  pl.BlockSpec((tb, PEER_ROWS), lambda i: (i, 0), memory_space=pltpu.SMEM),
                  pl.BlockSpec(v_pk.shape, lambda i: (0, 0, 0), pipeline_mode=pl.Buffered(1))],
        out_specs=pl.BlockSpec((tb, 8, LANES), lambda i: (i, 0, 0)),
        out_shape=jax.ShapeDtypeStruct((N, 8, LANES), jnp.float32),
        compiler_params=pltpu.CompilerParams(dimension_semantics=("arbitrary",),
                                             vmem_limit_bytes=_PEER_VMEM_LIMIT),
        name="peer_mix",
    )(idx, coef, v_pk)


def _peer_coef_kernel(a_ref, g_ref, c_ref):
    a = a_ref[...]
    gelu = 0.5 * a * (1.0 + jnp.tanh(math.sqrt(2.0 / math.pi) * (a + 0.044715 * a * a * a)))
    c_ref[...] = g_ref[...] * gelu


def peer_coef(act, g, tm):
    N, Rr = act.shape
    spec = pl.BlockSpec((tm, Rr), lambda i: (i, 0))
    return pl.pallas_call(
        _peer_coef_kernel, grid=(N // tm,), in_specs=[spec, spec], out_specs=spec,
        out_shape=jax.ShapeDtypeStruct((N, Rr), jnp.float32),
        compiler_params=_cparams(("arbitrary",)), name="peer_coef",
    )(act, g)


def _residual_kernel(x_ref, gate_ref, y_ref, g_ref, o_ref, *, final_norm):
    x = x_ref[...] + gate_ref[0] * y_ref[...]
    if final_norm:
        x = _rms(x, g_ref[...])
    o_ref[...] = x


def residual_out(x, gate, y, g_final, rows_per_group, tm, final_norm):
    N, D = x.shape
    G, R, _ = gate.shape
    if R == 1:
        bpg = rows_per_group // tm
        mod_spec = pl.BlockSpec((1, 1, D), lambda i: (i // bpg, 0, 0))
    else:
        mod_spec = pl.BlockSpec((1, tm, D), lambda i: (0, i, 0))
    spec = pl.BlockSpec((tm, D), lambda i: (i, 0))
    return pl.pallas_call(
        functools.partial(_residual_kernel, final_norm=final_norm),
        grid=(N // tm,),
        in_specs=[spec, mod_spec, spec, pl.BlockSpec((1, D), lambda i: (0, 0))],
        out_specs=spec,
        out_shape=jax.ShapeDtypeStruct((N, D), jnp.float32),
        compiler_params=_cparams(("arbitrary",)), name="residual_out",
    )(x, gate, y, g_final.reshape(1, D))


def pack_expert_rows(w):
    E, D = w.shape
    b = lax.bitcast_convert_type(w.astype(jnp.bfloat16), jnp.uint16).astype(jnp.uint32)
    pk = (b[:, :D // 2] << 16) | b[:, D // 2:]
    return pk.reshape(E, PK_SUB, LANES)


def rope_tables(pos):
    half = MLA_ROPE // 2
    freqs = ROPE_THETA ** (-jnp.arange(half, dtype=jnp.float32) / half)
    ang = pos.astype(jnp.float32)[:, None] * freqs[None, :]
    cos, sin = jnp.cos(ang), jnp.sin(ang)
    return jnp.tile(cos, (1, 8)), jnp.concatenate([-sin] * 4 + [sin] * 4, axis=1)


def mla_weights(w_uq, w_uk, w_uv):
    H, NP, hr = MLA_HEADS, MLA_NOPE, MLA_ROPE // 2
    nope = w_uq[:, :, :NP].reshape(MLA_Q_LORA, H * NP)
    x1 = w_uq[:, :, NP:NP + hr].reshape(MLA_Q_LORA, H * hr)
    x2 = w_uq[:, :, NP + hr:].reshape(MLA_Q_LORA, H * hr)
    w_uq_r = jnp.concatenate([nope, x1, x2], axis=1).astype(jnp.bfloat16)
    w_uk_bd = jnp.zeros((H * NP, H * MLA_KV_LORA), jnp.float32)
    w_uv_bd = jnp.zeros((H * MLA_KV_LORA, H * MLA_V), jnp.float32)
    for h in range(H):
        w_uk_bd = w_uk_bd.at[h * NP:(h + 1) * NP, h * MLA_KV_LORA:(h + 1) * MLA_KV_LORA].set(w_uk[:, h, :].T)
        w_uv_bd = w_uv_bd.at[h * MLA_KV_LORA:(h + 1) * MLA_KV_LORA, h * MLA_V:(h + 1) * MLA_V].set(w_uv[:, h, :])
    return w_uq_r, w_uk_bd.astype(jnp.bfloat16), w_uv_bd.astype(jnp.bfloat16)


_IN_SIZES = (256, 256, 256, 4, 512, 512, 512, 512, 4, 4, 256, 128, 32, 3072)


def pad_w_in(w_in):
    D = w_in.shape[0]
    offs = [0]
    for s in _IN_SIZES:
        offs.append(offs[-1] + s)
    fq, fk, fv, ff, mq, mk, mv, mo, mi, mf, cq, ckv, ckr, zg = (
        w_in[:, offs[i]:offs[i + 1]] for i in range(len(_IN_SIZES)))
    hr = MLA_ROPE // 2
    ckr_t = jnp.concatenate([ckr[:, :hr]] * 4 + [ckr[:, hr:]] * 4, axis=1)
    small = jnp.concatenate([ff, mi, mf, jnp.zeros((D, LANES - 12), w_in.dtype)], axis=1)
    pad = jnp.zeros((D, LANES), w_in.dtype)
    w = jnp.concatenate([zg, mq, mk, mv, mo, fq, fk, fv, cq, ckv, ckr_t, small, pad], axis=1)
    return w.astype(jnp.bfloat16)


def small_bias_row(b_fox_f, b_mlstm_i, b_mlstm_f):
    return jnp.concatenate([b_fox_f, b_mlstm_i, b_mlstm_f,
                            jnp.zeros((LANES - 12,), jnp.float32)]).reshape(1, LANES)


def _split2(w):
    h = w.astype(jnp.bfloat16)
    return h, (w - h.astype(jnp.float32)).astype(jnp.bfloat16)


def _krope_rows(krt):
    hr = MLA_ROPE // 2
    return jnp.concatenate([krt[:, :hr], krt[:, LANES // 2:LANES // 2 + hr]], axis=1)


def _token_major(a):
    nb, r, tm = a.shape
    return a.transpose(0, 2, 1).reshape(nb * tm, r)


PROMPT_TM = 512
ATTN_T_FOX = 512
ATTN_T_MLA = 256
SAMPLE_TM = 256
PEER_TB = 128
DECODE_PAGES_PER_STEP = 4
SAMPLE_CHUNK = 16


def _peer_ffn(xn2, lw, tm_sel):
    N, D = xn2.shape
    idx_t, g_t = peer_select(xn2, lw["wq_hi"], lw["wq_lo"], lw["sk_hi"], lw["sk_lo"], tm_sel)
    idx, g = _token_major(idx_t), _token_major(g_t)
    a8 = peer_act(idx, xn2.reshape(N, 8, LANES), lw["u_pk"], PEER_TB)
    act = a8[:, :, :PEER_ROWS // 8].transpose(0, 2, 1).reshape(N, PEER_ROWS)
    coef = peer_coef(act, g, tm_sel)
    return peer_mix(idx, coef, lw["v_pk"], PEER_TB).reshape(N, D)


def kernel(x_prompt, x_sample, c_prompt, c_sample, cache_fox_k, cache_fox_v, cache_fox_logf, cache_mla_ckv, cache_mla_krope, state_mlstm_C, state_mlstm_n, state_mlstm_m, page_table, w_ada, b_ada, g_norm_mix, g_norm_ffn, w_in, b_fox_f, b_mlstm_i, b_mlstm_f, g_mla_q, g_mla_kv, w_mla_uq, w_mla_uk, w_mla_uv, w_branch_fox, w_branch_mlstm, w_branch_mla, w_out, w_peer_q, peer_subkeys, peer_u, peer_v, g_final):
    B, S, D = x_prompt.shape
    DB, T, _ = x_sample.shape
    depth = w_in.shape[0]
    n_pool, page = cache_fox_k.shape[1], cache_fox_k.shape[2]
    n_pages = page_table.shape[1]
    past = n_pages * page
    Np, Ns = B * S, DB * T
    H = FOX_HEADS
    LP = SAMPLE_CHUNK

    xp = x_prompt.reshape(Np, D)
    xs = x_sample.reshape(Ns, D)
    n_c = B + DB
    c_all = jnp.concatenate([c_prompt, c_sample, jnp.zeros((-n_c % 8, D), jnp.float32)], axis=0)

    pool_k = cache_fox_k.reshape(depth * n_pool, page * H, FOX_HEAD_DIM)
    pool_v = cache_fox_v.reshape(depth * n_pool, page * H, FOX_HEAD_DIM)
    pool_w = pool_logf_suffix(cache_fox_logf.reshape(depth * n_pool, page * H), 512)
    pool_w = pool_w.reshape(depth * n_pool, 1, 2 * FOX_PAGE_COLS)
    pool_ckv = cache_mla_ckv.reshape(depth * n_pool, page, MLA_KV_LORA)
    pool_kr = cache_mla_krope.reshape(depth * n_pool, page, MLA_ROPE)
    cos_p, sin_p = rope_tables(jnp.arange(S))
    cos_s, sin_s = (jnp.tile(a, (DB, 1)) for a in rope_tables(past + jnp.arange(T)))

    prompt_rows, sample_rows = [], []
    for l in range(depth):
        w_uq_r, w_uk_bd, w_uv_bd = mla_weights(w_mla_uq[l], w_mla_uk[l], w_mla_uv[l])
        wq_hi, wq_lo = _split2(w_peer_q[l].T)
        sk_hi, sk_lo = _split2(peer_subkeys[l].reshape(2 * PEER_HEADS, PEER_N_KEYS, PEER_KEY_DIM // 2))
        lw = dict(wq_hi=wq_hi, wq_lo=wq_lo, sk_hi=sk_hi, sk_lo=sk_lo,
                  u_pk=pack_expert_rows(peer_u[l]), v_pk=pack_expert_rows(peer_v[l]))
        w_bf = pad_w_in(w_in[l])
        bias_row = small_bias_row(b_fox_f[l], b_mlstm_i[l], b_mlstm_f[l])
        wbf, wbm, wbc, wout = (_bf(w_branch_fox[l]), _bf(w_branch_mlstm[l]),
                               _bf(w_branch_mla[l]), _bf(w_out[l]))
        last = l == depth - 1

        mods = ada_ln(c_all, w_ada[l], b_ada[l])
        mp = mods[:B].reshape(B, 6, D)
        ms = mods[B:n_c].reshape(DB, 6, D)
        mod_p = [mp[:, i][:, None, :] for i in range(6)]
        mod_s = [jnp.repeat(ms[:, i], T, axis=0)[None] for i in range(6)]

        z = in_proj(xp, mod_p[0], mod_p[1], g_norm_mix[l], w_bf, S, PROMPT_TM)
        sm, cum = small_gates(z, bias_row, B, S, 256)
        cum_t = jnp.pad(jnp.swapaxes(cum[:, :H].reshape(B, S, H), 1, 2), ((0, 0), (0, 8 - H), (0, 0)))
        o_fox = fox_prompt(z, cum, cum_t, B, S, ATTN_T_FOX)
        qc, kc, ckvn, krt, _ = mla_prep(z, cos_p, sin_p, S // PROMPT_TM, g_mla_q[l], w_uq_r, w_uk_bd,
                                        g_mla_kv[l], PROMPT_TM)
        o_lat = mla_prompt(qc, kc, B, S, ATTN_T_MLA)
        nc = S // MLSTM_CHUNK
        hw = MLSTM_HEADS * MLSTM_DK
        kt = z[:, MK_OFF:MK_OFF + hw].reshape(B, S, MLSTM_HEADS, MLSTM_DK).transpose(0, 2, 3, 1)
        smt = sm[:, :16].reshape(B * nc, MLSTM_CHUNK, 16).transpose(0, 2, 1)
        h_ml, C_p, n_p, m_p = mlstm_scan(z, z, z, (MQ_OFF // hw, MK_OFF // hw, MV_OFF // hw), kt, sm, smt,
                                         None, B, nc, MLSTM_CHUNK, MLSTM_CHUNK)
        x1, xn2 = merge_block(xp, mod_p[2], mod_p[3], mod_p[4], o_fox, h_ml, z, o_lat,
                              wbf, wbm, w_uv_bd, wbc, wout, g_norm_ffn[l], S, 256)
        y = _peer_ffn(xn2, lw, 256)
        xp = residual_out(x1, mod_p[5], y, g_final, S, PROMPT_TM, last)
        prompt_rows.append((
            z[:, FK_OFF:FK_OFF + 256].reshape(B, S, H, FOX_HEAD_DIM),
            z[:, FV_OFF:FV_OFF + 256].reshape(B, S, H, FOX_HEAD_DIM),
            sm[:, :H].reshape(B, S, H),
            ckvn.reshape(B, S, MLA_KV_LORA),
            _krope_rows(krt).reshape(B, S, MLA_ROPE),
            C_p, n_p[:, :, 0, :], m_p[:, :, 0, 0]))

        zs = in_proj(xs, mod_s[0], mod_s[1], g_norm_mix[l], w_bf, Ns, SAMPLE_TM)
        sm_s, _ = small_gates(zs, bias_row, 1, Ns, SAMPLE_TM)
        fq = zs[:, FQ_OFF:FQ_OFF + 256].reshape(DB, T, H, FOX_HEAD_DIM).transpose(0, 2, 1, 3)
        fk_s = zs[:, FK_OFF:FK_OFF + 256]
        fv_s = zs[:, FV_OFF:FV_OFF + 256]
        lf_s = sm_s[:, :H]
        o_f = fox_decode(page_table, fq.reshape(DB, H * T, FOX_HEAD_DIM),
                         fk_s.reshape(DB, T * H, FOX_HEAD_DIM), fv_s.reshape(DB, T * H, FOX_HEAD_DIM),
                         lf_s.reshape(DB, 1, T * H), lf_s.reshape(DB, T * H, 1),
                         pool_k, pool_v, pool_w, l * n_pool, DECODE_PAGES_PER_STEP)
        o_fox_s = o_f.reshape(DB, H, T, FOX_HEAD_DIM).transpose(0, 2, 1, 3).reshape(Ns, H * FOX_HEAD_DIM)
        qc_s, _, ckvn_s, krt_s, qrt_s = mla_prep(zs, cos_s, sin_s, Ns // SAMPLE_TM, g_mla_q[l], w_uq_r,
                                                 w_uk_bd, g_mla_kv[l], SAMPLE_TM)
        q_lat_s = qc_s[:, :, :MLA_KV_LORA].reshape(MLA_HEADS, DB, T, MLA_KV_LORA).transpose(1, 0, 2, 3)
        hr = MLA_ROPE // 2
        q_rope_s = qrt_s.reshape(DB, T, 2, MLA_HEADS, hr).transpose(0, 3, 1, 2, 4)
        kr_s = _krope_rows(krt_s)
        o_l = mla_decode(page_table, q_lat_s.reshape(DB, MLA_HEADS * T, MLA_KV_LORA),
                         q_rope_s.reshape(DB, MLA_HEADS * T, MLA_ROPE),
                         ckvn_s.reshape(DB, T, MLA_KV_LORA), kr_s.reshape(DB, T, MLA_ROPE),
                         pool_ckv, pool_kr, l * n_pool, DECODE_PAGES_PER_STEP)
        o_lat_s = o_l.reshape(DB, MLA_HEADS, T, MLA_KV_LORA).transpose(1, 0, 2, 3).reshape(MLA_HEADS, Ns, MLA_KV_LORA)

        def pad_rows(a):
            a = a.reshape(DB, T, a.shape[-1])
            return jnp.pad(a, ((0, 0), (0, LP - T), (0, 0))).reshape(DB * LP, a.shape[-1])

        mq_p = pad_rows(zs[:, MQ_OFF:MQ_OFF + hw])
        mk_p = pad_rows(zs[:, MK_OFF:MK_OFF + hw])
        mv_p = pad_rows(zs[:, MV_OFF:MV_OFF + hw])
        sm_p = pad_rows(sm_s)
        kt_s = mk_p.reshape(DB, LP, MLSTM_HEADS, MLSTM_DK).transpose(0, 2, 3, 1)
        smt_s = sm_p[:, :16].reshape(DB, LP, 16).transpose(0, 2, 1)
        state = (state_mlstm_C[l], state_mlstm_n[l][:, :, None, :],
                 jnp.broadcast_to(state_mlstm_m[l][:, :, None, None], (DB, MLSTM_HEADS, 1, LANES)))
        h_s, C_s, n_s, m_s = mlstm_scan(mq_p, mk_p, mv_p, (0, 0, 0), kt_s, sm_p, smt_s, state, DB, 1, LP, T)
        h_ml_s = h_s.reshape(DB, LP, hw)[:, :T].reshape(Ns, hw)
        x1s, xn2s = merge_block(xs, mod_s[2], mod_s[3], mod_s[4], o_fox_s, h_ml_s, zs, o_lat_s,
                                wbf, wbm, w_uv_bd, wbc, wout, g_norm_ffn[l], Ns, SAMPLE_TM)
        ys = _peer_ffn(xn2s, lw, SAMPLE_TM)
        xs = residual_out(x1s, mod_s[5], ys, g_final, Ns, SAMPLE_TM, last)
        sample_rows.append((
            fk_s.reshape(DB, T, H, FOX_HEAD_DIM), fv_s.reshape(DB, T, H, FOX_HEAD_DIM),
            lf_s.reshape(DB, T, H), ckvn_s.reshape(DB, T, MLA_KV_LORA), kr_s.reshape(DB, T, MLA_ROPE),
            C_s, n_s[:, :, 0, :], m_s[:, :, 0, 0]))

    p_out = [jnp.stack(r) for r in zip(*prompt_rows)]
    s_out = [jnp.stack(r) for r in zip(*sample_rows)]
    return (xp.reshape(B, S, D), xs.reshape(DB, T, D), *p_out, *s_out)
```

```python
import functools
import math

import jax
import jax.numpy as jnp
from jax import lax
from jax.experimental import pallas as pl
from jax.experimental.pallas import tpu as pltpu

D_MODEL = 1024
FOX_HEADS = 4
FOX_HEAD_DIM = 64
MLSTM_HEADS = 4
MLSTM_DK = 128
MLSTM_DV = 128
MLSTM_CHUNK = 128
MLA_HEADS = 4
MLA_Q_LORA = 256
MLA_KV_LORA = 128
MLA_NOPE = 64
MLA_ROPE = 32
MLA_V = 64
MLA_SCALE = (MLA_NOPE + MLA_ROPE) ** -0.5
ROPE_THETA = 10000.0
PEER_HEADS = 8
PEER_N_KEYS = 128
PEER_KEY_DIM = 128
PEER_TOPK = 16
NORM_EPS = 1e-6
N_BRANCH = 3

LANES = 128
NEG_BIG = -1e30

ZG_OFF = 0
MQ_OFF = 3072
MK_OFF = MQ_OFF + 512
MV_OFF = MK_OFF + 512
MO_OFF = MV_OFF + 512
FQ_OFF = 5120
FK_OFF = FQ_OFF + 256
FV_OFF = FK_OFF + 256
CQ_OFF = 5888
CKV_OFF = 6144
CKR_OFF = 6272
SM_OFF = 6400
Z_COLS = 6656
SM_FOXF = 0
SM_MI = 4
SM_MF = 8

_VMEM_LIMIT = 48 * 1024 * 1024
_PEER_VMEM_LIMIT = 44 * 1024 * 1024


def _cparams(sem):
    return pltpu.CompilerParams(dimension_semantics=sem, vmem_limit_bytes=_VMEM_LIMIT)


def _bf(x):
    return x.astype(jnp.bfloat16)


def _dot(a, b):
    return jnp.dot(a, b, preferred_element_type=jnp.float32)


def _dot_nt(a, b):
    return lax.dot_general(a, b, (((1,), (1,)), ((), ())), preferred_element_type=jnp.float32)


def _split3(x):
    h = _bf(x)
    r = x - h.astype(jnp.float32)
    m = _bf(r)
    l = _bf(r - m.astype(jnp.float32))
    return h, m, l


def _log_sigmoid(x):
    return jnp.minimum(x, 0.0) - jnp.log1p(jnp.exp(-jnp.abs(x)))


def _ada_kernel(c_ref, w_ref, b_ref, o_ref):
    c = c_ref[...]
    a = c * (1.0 / (1.0 + jnp.exp(-c)))
    o_ref[...] = _dot(_bf(a), _bf(w_ref[...])) + b_ref[...]


def ada_ln(c, w, b):
    R, D = c.shape
    N = w.shape[1]
    tn = 1024
    return pl.pallas_call(
        _ada_kernel,
        grid=(N // tn,),
        in_specs=[pl.BlockSpec((R, D), lambda j: (0, 0)),
                  pl.BlockSpec((D, tn), lambda j: (0, j)),
                  pl.BlockSpec((1, tn), lambda j: (0, j))],
        out_specs=pl.BlockSpec((R, tn), lambda j: (0, j)),
        out_shape=jax.ShapeDtypeStruct((R, N), jnp.float32),
        compiler_params=_cparams(("arbitrary",)),
        name="ada_ln",
    )(c, w, b.reshape(1, N))


def _modulate(x, g, shift, scale):
    y = x * lax.rsqrt(jnp.mean(x * x, axis=-1, keepdims=True) + NORM_EPS)
    return (y * g) * (1.0 + scale) + shift


def _inproj_kernel(x_ref, sh_ref, sc_ref, g_ref, w_ref, z_ref, xn_sc):
    @pl.when(pl.program_id(1) == 0)
    def _():
        xn_sc[...] = _bf(_modulate(x_ref[...], g_ref[...], sh_ref[0], sc_ref[0]))

    z_ref[...] = _dot(xn_sc[...], w_ref[...])


def in_proj(x, shift, scale, g, w_bf, rows_per_group, tm):
    N, D = x.shape
    G, R, _ = shift.shape
    tn = 512
    if R == 1:
        bpg = rows_per_group // tm
        mod_spec = pl.BlockSpec((1, 1, D), lambda i, j: (i // bpg, 0, 0))
    else:
        mod_spec = pl.BlockSpec((1, tm, D), lambda i, j: (0, i, 0))
    return pl.pallas_call(
        _inproj_kernel,
        grid=(N // tm, Z_COLS // tn),
        in_specs=[pl.BlockSpec((tm, D), lambda i, j: (i, 0)),
                  mod_spec, mod_spec,
                  pl.BlockSpec((1, D), lambda i, j: (0, 0)),
                  pl.BlockSpec((D, tn), lambda i, j: (0, j))],
        out_specs=pl.BlockSpec((tm, tn), lambda i, j: (i, j)),
        out_shape=jax.ShapeDtypeStruct((N, Z_COLS), jnp.float32),
        scratch_shapes=[pltpu.VMEM((tm, D), jnp.bfloat16)],
        compiler_params=_cparams(("arbitrary", "arbitrary")),
        name="in_proj",
    )(x, shift, scale, g.reshape(1, D), w_bf)


def _small_kernel(z_ref, bias_ref, sm_ref, cum_ref, carry_sc, *, tb):
    @pl.when(pl.program_id(1) == 0)
    def _():
        carry_sc[...] = jnp.zeros_like(carry_sc)

    v = z_ref[...] + bias_ref[...]
    lane = lax.broadcasted_iota(jnp.int32, v.shape, 1)
    is_ls = (lane < SM_MI) | ((lane >= SM_MF) & (lane < SM_MF + MLSTM_HEADS))
    sm = jnp.where(is_ls, _log_sigmoid(v), v)
    sm_ref[...] = sm
    row = lax.broadcasted_iota(jnp.int32, (tb, tb), 0)
    col = lax.broadcasted_iota(jnp.int32, (tb, tb), 1)
    tri = _bf(jnp.where(row >= col, 1.0, 0.0))
    h, m, l = _split3(sm)
    cum = _dot(tri, h) + _dot(tri, m) + _dot(tri, l) + carry_sc[...]
    cum_ref[...] = cum
    carry_sc[...] = cum[tb - 1:tb, :]


def small_gates(z, bias_row, n_groups, rows_per_group, tb):
    N = z.shape[0]
    nb = rows_per_group // tb
    cb = SM_OFF // LANES
    return pl.pallas_call(
        functools.partial(_small_kernel, tb=tb),
        grid=(n_groups, nb),
        in_specs=[pl.BlockSpec((tb, LANES), lambda b, i: (b * nb + i, cb)),
                  pl.BlockSpec((1, LANES), lambda b, i: (0, 0))],
        out_specs=[pl.BlockSpec((tb, LANES), lambda b, i: (b * nb + i, 0)),
                   pl.BlockSpec((tb, LANES), lambda b, i: (b * nb + i, 0))],
        out_shape=[jax.ShapeDtypeStruct((N, LANES), jnp.float32),
                   jax.ShapeDtypeStruct((N, LANES), jnp.float32)],
        scratch_shapes=[pltpu.VMEM((1, LANES), jnp.float32)],
        compiler_params=_cparams(("arbitrary", "arbitrary")),
        name="small_gates",
    )(z, bias_row)


def _fox_prompt_kernel(q_ref, k_ref, v_ref, fq_ref, fk_ref, o_ref, m_sc, l_sc, acc_sc, *, t):
    qi = pl.program_id(1)
    ki = pl.program_id(2)

    @pl.when(ki == 0)
    def _():
        m_sc[...] = jnp.full_like(m_sc, NEG_BIG)
        l_sc[...] = jnp.zeros_like(l_sc)
        acc_sc[...] = jnp.zeros_like(acc_sc)

    def step(diagonal):
        scale = FOX_HEAD_DIM ** -0.5
        fq = fq_ref[...]
        fk = fk_ref[0]
        half = lax.broadcasted_iota(jnp.int32, (1, LANES), 1) // FOX_HEAD_DIM
        for h in range(FOX_HEADS):
            sl = slice((h // 2) * LANES, (h // 2 + 1) * LANES)
            mine = half == (h % 2)
            qm = jnp.where(mine, q_ref[:, sl], 0.0)
            s = _dot_nt(_bf(qm), _bf(k_ref[:, sl])) * scale
            s = s + fq[:, h:h + 1] - fk[h:h + 1, :]
            if diagonal:
                row = lax.broadcasted_iota(jnp.int32, (t, t), 0)
                col = lax.broadcasted_iota(jnp.int32, (t, t), 1)
                s = jnp.where(row >= col, s, NEG_BIG)
            m_prev = m_sc[h]
            m_new = jnp.maximum(m_prev, jnp.max(s, axis=-1, keepdims=True))
            p = jnp.exp(s - jnp.tile(m_new, (1, t // LANES)))
            alpha = jnp.exp(m_prev - m_new)
            l_sc[h] = alpha * l_sc[h] + jnp.sum(p, axis=-1, keepdims=True)
            pv = _dot(_bf(p), _bf(v_ref[:, sl]))
            acc = acc_sc[h // 2]
            acc_sc[h // 2] = jnp.where(mine, alpha * acc + pv, acc)
            m_sc[h] = m_new

    @pl.when(ki < qi)
    def _():
        step(False)

    @pl.when(ki == qi)
    def _():
        step(True)
        half = lax.broadcasted_iota(jnp.int32, (1, LANES), 1) // FOX_HEAD_DIM
        for pr in range(FOX_HEADS // 2):
            inv = jnp.where(half == 0, 1.0 / l_sc[2 * pr], 1.0 / l_sc[2 * pr + 1])
            o_ref[:, pr * LANES:(pr + 1) * LANES] = acc_sc[pr] * inv


def fox_prompt(z, cum, cum_t, B, S, t):
    nq = S // t
    hd = FOX_HEADS * FOX_HEAD_DIM
    qb, kb, vb = FQ_OFF // hd, FK_OFF // hd, FV_OFF // hd
    return pl.pallas_call(
        functools.partial(_fox_prompt_kernel, t=t),
        grid=(B, nq, nq),
        in_specs=[pl.BlockSpec((t, hd), lambda b, i, j: (b * nq + i, qb)),
                  pl.BlockSpec((t, hd), lambda b, i, j: (b * nq + jnp.minimum(i, j), kb)),
                  pl.BlockSpec((t, hd), lambda b, i, j: (b * nq + jnp.minimum(i, j), vb)),
                  pl.BlockSpec((t, LANES), lambda b, i, j: (b * nq + i, 0)),
                  pl.BlockSpec((1, 8, t), lambda b, i, j: (b, 0, jnp.minimum(i, j)))],
        out_specs=pl.BlockSpec((t, hd), lambda b, i, j: (b * nq + i, 0)),
        out_shape=jax.ShapeDtypeStruct((B * S, hd), jnp.float32),
        scratch_shapes=[pltpu.VMEM((FOX_HEADS, t, LANES), jnp.float32),
                        pltpu.VMEM((FOX_HEADS, t, LANES), jnp.float32),
                        pltpu.VMEM((FOX_HEADS // 2, t, LANES), jnp.float32)],
        compiler_params=_cparams(("arbitrary", "arbitrary", "arbitrary")),
        name="fox_prompt",
    )(z, z, z, cum, cum_t)


def _rms(x, g):
    return x * lax.rsqrt(jnp.mean(x * x, axis=-1, keepdims=True) + NORM_EPS) * g


def _rope_tiled(v, cos_t, sin_t):
    return v * cos_t + pltpu.roll(v, LANES // 2, axis=1) * sin_t


def _mla_prep_kernel(cq_ref, ckv_ref, ckr_ref, cos_ref, sin_ref, gq_ref, wuq_ref, wuk_ref, gkv_ref,
                     qc_ref, kc_ref, ckvn_ref, krt_ref, qrt_ref):
    cos_t = cos_ref[...]
    sin_t = sin_ref[...]
    cqn = _rms(cq_ref[...], gq_ref[...])
    qf = _dot(_bf(cqn), wuq_ref[...])
    nq = MLA_HEADS * MLA_NOPE
    q_lat = _dot(_bf(qf[:, :nq]), wuk_ref[...])
    q_rope = _rope_tiled(qf[:, nq:], cos_t, sin_t)
    qrt_ref[...] = q_rope
    lane = lax.broadcasted_iota(jnp.int32, (1, LANES), 1)
    owner = (lane % (LANES // 2)) // (MLA_ROPE // 2)
    for h in range(MLA_HEADS):
        qc_ref[h, :, :MLA_KV_LORA] = _bf(q_lat[:, h * MLA_KV_LORA:(h + 1) * MLA_KV_LORA])
        qc_ref[h, :, MLA_KV_LORA:] = _bf(jnp.where(owner == h, q_rope, 0.0))
    ckvn = _rms(ckv_ref[...], gkv_ref[...])
    ckvn_ref[...] = ckvn
    k_rope = _rope_tiled(ckr_ref[...], cos_t, sin_t)
    krt_ref[...] = k_rope
    kc_ref[:, :MLA_KV_LORA] = _bf(ckvn)
    kc_ref[:, MLA_KV_LORA:] = _bf(k_rope)


def mla_prep(z, cos_t, sin_t, table_blocks, g_q, w_uq_r, w_uk_bd, g_kv, tm):
    N = z.shape[0]
    cw = 2 * MLA_KV_LORA
    tab = pl.BlockSpec((tm, LANES), lambda i: (i % table_blocks, 0))
    full = lambda a: pl.BlockSpec(a.shape, lambda i: (0,) * a.ndim)
    gq = g_q.reshape(1, -1)
    gkv = g_kv.reshape(1, -1)
    return pl.pallas_call(
        _mla_prep_kernel,
        grid=(N // tm,),
        in_specs=[pl.BlockSpec((tm, MLA_Q_LORA), lambda i: (i, CQ_OFF // MLA_Q_LORA)),
                  pl.BlockSpec((tm, LANES), lambda i: (i, CKV_OFF // LANES)),
                  pl.BlockSpec((tm, LANES), lambda i: (i, CKR_OFF // LANES)),
                  tab, tab, full(gq), full(w_uq_r), full(w_uk_bd), full(gkv)],
        out_specs=[pl.BlockSpec((MLA_HEADS, tm, cw), lambda i: (0, i, 0)),
                   pl.BlockSpec((tm, cw), lambda i: (i, 0)),
                   pl.BlockSpec((tm, LANES), lambda i: (i, 0)),
                   pl.BlockSpec((tm, LANES), lambda i: (i, 0)),
                   pl.BlockSpec((tm, LANES), lambda i: (i, 0))],
        out_shape=[jax.ShapeDtypeStruct((MLA_HEADS, N, cw), jnp.bfloat16),
                   jax.ShapeDtypeStruct((N, cw), jnp.bfloat16),
                   jax.ShapeDtypeStruct((N, LANES), jnp.float32),
                   jax.ShapeDtypeStruct((N, LANES), jnp.float32),
                   jax.ShapeDtypeStruct((N, LANES), jnp.float32)],
        compiler_params=_cparams(("arbitrary",)),
        name="mla_prep",
    )(z, z, z, cos_t, sin_t, gq, w_uq_r, w_uk_bd, gkv)


def _mla_prompt_kernel(q_ref, k_ref, o_ref, m_sc, l_sc, acc_sc, *, t):
    qi = pl.program_id(1)
    ki = pl.program_id(2)
    rows = MLA_HEADS * t

    @pl.when(ki == 0)
    def _():
        m_sc[...] = jnp.full_like(m_sc, NEG_BIG)
        l_sc[...] = jnp.zeros_like(l_sc)
        acc_sc[...] = jnp.zeros_like(acc_sc)

    def step(diagonal):
        q = q_ref[...].reshape(rows, 2 * MLA_KV_LORA)
        k = k_ref[...]
        s = _dot_nt(q, k) * MLA_SCALE
        if diagonal:
            row = lax.broadcasted_iota(jnp.int32, (rows, t), 0) & (t - 1)
            col = lax.broadcasted_iota(jnp.int32, (rows, t), 1)
            s = jnp.where(row >= col, s, NEG_BIG)
        m_prev = m_sc[...]
        m_new = jnp.maximum(m_prev, jnp.max(s, axis=-1, keepdims=True))
        p = jnp.exp(s - jnp.tile(m_new, (1, t // LANES)))
        alpha = jnp.exp(m_prev - m_new)
        l_sc[...] = alpha * l_sc[...] + jnp.sum(p, axis=-1, keepdims=True)
        acc_sc[...] = alpha * acc_sc[...] + _dot(_bf(p), k[:, :MLA_KV_LORA])
        m_sc[...] = m_new

    @pl.when(ki < qi)
    def _():
        step(False)

    @pl.when(ki == qi)
    def _():
        step(True)
        o_ref[...] = (acc_sc[...] / l_sc[...]).reshape(MLA_HEADS, t, MLA_KV_LORA)


def mla_prompt(qc, kc, B, S, t):
    nq = S // t
    cw = 2 * MLA_KV_LORA
    rows = MLA_HEADS * t
    return pl.pallas_call(
        functools.partial(_mla_prompt_kernel, t=t),
        grid=(B, nq, nq),
        in_specs=[pl.BlockSpec((MLA_HEADS, t, cw), lambda b, i, j: (0, b * nq + i, 0)),
                  pl.BlockSpec((t, cw), lambda b, i, j: (b * nq + jnp.minimum(i, j), 0))],
        out_specs=pl.BlockSpec((MLA_HEADS, t, MLA_KV_LORA), lambda b, i, j: (0, b * nq + i, 0)),
        out_shape=jax.ShapeDtypeStruct((MLA_HEADS, B * S, MLA_KV_LORA), jnp.float32),
        scratch_shapes=[pltpu.VMEM((rows, LANES), jnp.float32),
                        pltpu.VMEM((rows, LANES), jnp.float32),
                        pltpu.VMEM((rows, MLA_KV_LORA), jnp.float32)],
        compiler_params=_cparams(("arbitrary", "arbitrary", "arbitrary")),
        name="mla_prompt",
    )(qc, kc)


def _mlstm_kernel(*refs, L, l_real, zero_init):
    if zero_init:
        q_ref, k_ref, v_ref, kt_ref, sm_ref, smt_ref, h_ref, c_ref, n_ref, m_ref, c_sc, n_sc, m_sc = refs
    else:
        (q_ref, k_ref, v_ref, kt_ref, sm_ref, smt_ref, c0_ref, n0_ref, m0_ref,
         h_ref, c_ref, n_ref, m_ref, c_sc, n_sc, m_sc) = refs
    ci = pl.program_id(1)

    @pl.when(ci == 0)
    def _():
        if zero_init:
            c_sc[...] = jnp.zeros_like(c_sc)
            n_sc[...] = jnp.zeros_like(n_sc)
            m_sc[...] = jnp.zeros_like(m_sc)
        else:
            c_sc[...] = c0_ref[0]
            n_sc[...] = n0_ref[0]
            m_sc[...] = m0_ref[0]

    kscale = MLSTM_DK ** -0.5
    row = lax.broadcasted_iota(jnp.int32, (L, L), 0)
    col = lax.broadcasted_iota(jnp.int32, (L, L), 1)
    tri = row >= col
    real_r = lax.broadcasted_iota(jnp.int32, (1, L), 1) < l_real
    real_c = lax.broadcasted_iota(jnp.int32, (L, 1), 0) < l_real
    sm = sm_ref[...]
    smt = smt_ref[0]
    last = l_real - 1
    for h in range(MLSTM_HEADS):
        sl = slice(h * MLSTM_DK, (h + 1) * MLSTM_DK)
        qh = q_ref[:, sl]
        kh = k_ref[:, sl] * kscale
        vh = _bf(v_ref[:, sl])
        kth = kt_ref[0, h] * kscale
        ig_r = smt[SM_MI + h:SM_MI + h + 1, :]
        lf_r = smt[SM_MF + h:SM_MF + h + 1, :]
        ig_c = sm[:, SM_MI + h:SM_MI + h + 1]
        lf_c = sm[:, SM_MF + h:SM_MF + h + 1]
        b_c = jnp.sum(jnp.where(tri, lf_r, 0.0), axis=1, keepdims=True)
        b_r = jnp.sum(jnp.where(col >= row, lf_c, 0.0), axis=0, keepdims=True)
        d = jnp.where(tri, b_c - b_r + ig_r, NEG_BIG)
        m_prev = m_sc[h][:, :1]
        inter = b_c + m_prev
        m_t = jnp.maximum(inter, jnp.max(d, axis=1, keepdims=True))
        w = jnp.exp(d - m_t)
        a = jnp.exp(inter - m_t)
        qb = _bf(qh)
        qk = _dot_nt(qb, _bf(kh)) * w
        c_prev = c_sc[h]
        n_prev = n_sc[h]
        num = a * _dot(qb, _bf(c_prev)) + _dot(_bf(qk), vh)
        den = a * jnp.sum(qh * n_prev, axis=1, keepdims=True) + jnp.sum(qk, axis=1, keepdims=True)
        h_ref[:, sl] = num / jnp.maximum(jnp.abs(den), jnp.exp(-m_t))
        m_new = m_t[last:last + 1, :]
        b_last = b_c[last:last + 1, :]
        a_end = jnp.exp(b_last + m_prev - m_new)
        w_r = jnp.where(real_r, jnp.exp(b_last - b_r + ig_r - m_new), 0.0)
        w_c = jnp.where(real_c, jnp.exp(b_last - b_c + ig_c - m_new), 0.0)
        c_sc[h] = a_end * c_prev + _dot(_bf(kth * w_r), vh)
        n_sc[h] = a_end * n_prev + jnp.sum(w_c * kh, axis=0, keepdims=True)
        m_sc[h] = jnp.broadcast_to(m_new, (1, LANES))

    @pl.when(ci == pl.num_programs(1) - 1)
    def _():
        c_ref[0] = c_sc[...]
        n_ref[0] = n_sc[...]
        m_ref[0] = m_sc[...]


def mlstm_scan(q, k, v, colblocks, kt, sm, smt, state, Bm, nc, L, l_real):
    H, DK, DV = MLSTM_HEADS, MLSTM_DK, MLSTM_DV
    zero_init = state is None
    rows = Bm * nc * L
    hw = H * DK
    qs, ks_, vs = (pl.BlockSpec((L, hw), functools.partial(lambda b, c, cb: (b * nc + c, cb), cb=cb))
                   for cb in colblocks)
    in_specs = [qs, ks_, vs,
                pl.BlockSpec((1, H, DK, L), lambda b, c: (b, 0, 0, c)),
                pl.BlockSpec((L, LANES), lambda b, c: (b * nc + c, 0)),
                pl.BlockSpec((1, 16, L), lambda b, c: (b * nc + c, 0, 0))]
    args = [q, k, v, kt, sm, smt]
    st_specs = [pl.BlockSpec((1, H, DK, DV), lambda b, c: (b, 0, 0, 0)),
                pl.BlockSpec((1, H, 1, DK), lambda b, c: (b, 0, 0, 0)),
                pl.BlockSpec((1, H, 1, LANES), lambda b, c: (b, 0, 0, 0))]
    if not zero_init:
        in_specs += st_specs
        args += list(state)
    return pl.pallas_call(
        functools.partial(_mlstm_kernel, L=L, l_real=l_real, zero_init=zero_init),
        grid=(Bm, nc),
        in_specs=in_specs,
        out_specs=[pl.BlockSpec((L, hw), lambda b, c: (b * nc + c, 0))] + st_specs,
        out_shape=[jax.ShapeDtypeStruct((rows, hw), jnp.float32),
                   jax.ShapeDtypeStruct((Bm, H, DK, DV), jnp.float32),
                   jax.ShapeDtypeStruct((Bm, H, 1, DK), jnp.float32),
                   jax.ShapeDtypeStruct((Bm, H, 1, LANES), jnp.float32)],
        scratch_shapes=[pltpu.VMEM((H, DK, DV), jnp.float32),
                        pltpu.VMEM((H, 1, DK), jnp.float32),
                        pltpu.VMEM((H, 1, LANES), jnp.float32)],
        compiler_params=_cparams(("arbitrary", "arbitrary")),
        name="mlstm_scan",
    )(*args)


def _merge_kernel(x_ref, gate_ref, sh2_ref, sc2_ref, ofox_ref, hml_ref, mo_ref, zg0_ref, zg1_ref, zg2_ref,
                  olat_ref, wbf_ref, wbm_ref, wuv_ref, wbc_ref, wout_ref, g2_ref, x1_ref, xn2_ref):
    sig = lambda u: 1.0 / (1.0 + jnp.exp(-u))
    o_ml = sig(mo_ref[...]) * hml_ref[...]
    o_lat = jnp.concatenate([olat_ref[h] for h in range(MLA_HEADS)], axis=1)
    o_mla = _dot(_bf(o_lat), wuv_ref[...])
    merged = (sig(zg0_ref[...]) * _dot(_bf(ofox_ref[...]), wbf_ref[...])
              + sig(zg1_ref[...]) * _dot(_bf(o_ml), wbm_ref[...])
              + sig(zg2_ref[...]) * _dot(_bf(o_mla), wbc_ref[...]))
    x1 = x_ref[...] + gate_ref[0] * _dot(_bf(merged), wout_ref[...])
    x1_ref[...] = x1
    xn2_ref[...] = _modulate(x1, g2_ref[...], sh2_ref[0], sc2_ref[0])


def merge_block(x, gate, shift2, scale2, o_fox, h_ml, z, o_lat, wbf, wbm, wuv_bd, wbc, wout, g2,
                rows_per_group, tm):
    N, D = x.shape
    G, R, _ = gate.shape
    if R == 1:
        bpg = rows_per_group // tm
        mod_spec = pl.BlockSpec((1, 1, D), lambda i: (i // bpg, 0, 0))
    else:
        mod_spec = pl.BlockSpec((1, tm, D), lambda i: (0, i, 0))
    full = lambda a: pl.BlockSpec(a.shape, lambda i: (0,) * a.ndim)
    rowblk = lambda w, cb=0: pl.BlockSpec((tm, w), lambda i: (i, cb))
    g2r = g2.reshape(1, D)
    return pl.pallas_call(
        _merge_kernel,
        grid=(N // tm,),
        in_specs=[rowblk(D), mod_spec, mod_spec, mod_spec,
                  rowblk(256), rowblk(512), rowblk(512, MO_OFF // 512),
                  rowblk(D, 0), rowblk(D, 1), rowblk(D, 2),
                  pl.BlockSpec((MLA_HEADS, tm, MLA_KV_LORA), lambda i: (0, i, 0)),
                  full(wbf), full(wbm), full(wuv_bd), full(wbc), full(wout), full(g2r)],
        out_specs=[rowblk(D), rowblk(D)],
        out_shape=[jax.ShapeDtypeStruct((N, D), jnp.float32),
                   jax.ShapeDtypeStruct((N, D), jnp.float32)],
        compiler_params=_cparams(("arbitrary",)),
        name="merge_block",
    )(x, gate, shift2, scale2, o_fox, h_ml, z, z, z, z, o_lat, wbf, wbm, wuv_bd, wbc, wout, g2r)


FOX_PAGE_COLS = 512


def _pool_logf_kernel(x_ref, w_ref):
    n = FOX_PAGE_COLS
    r = lax.broadcasted_iota(jnp.int32, (n, 2 * n), 0)
    c = lax.broadcasted_iota(jnp.int32, (n, 2 * n), 1)
    same_head = (r % FOX_HEADS) == (c % FOX_HEADS)
    sel = same_head & ((c >= n) | (r > c))
    mat = _bf(jnp.where(sel, 1.0, 0.0))
    h, m, l = _split3(x_ref[...])
    w_ref[...] = _dot(h, mat) + _dot(m, mat) + _dot(l, mat)


def pool_logf_suffix(logf_flat, tm):
    n_rows = logf_flat.shape[0]
    n = FOX_PAGE_COLS
    assert n_rows % tm == 0 and logf_flat.shape[1] == n
    return pl.pallas_call(
        _pool_logf_kernel,
        grid=(n_rows // tm,),
        in_specs=[pl.BlockSpec((tm, n), lambda i: (i, 0))],
        out_specs=pl.BlockSpec((tm, 2 * n), lambda i: (i, 0)),
        out_shape=jax.ShapeDtypeStruct((n_rows, 2 * n), jnp.float32),
        compiler_params=_cparams(("arbitrary",)),
        name="pool_logf_suffix",
    )(logf_flat)


def _online_update_many(scores, values_bf, m_sc, l_sc, acc_sc):
    m_prev = m_sc[...]
    m_new = m_prev
    for s in scores:
        m_new = jnp.maximum(m_new, jnp.max(s, axis=-1, keepdims=True))
    alpha = jnp.exp(m_prev - m_new)
    l = alpha * l_sc[...]
    acc = alpha * acc_sc[...]
    for s, v_bf in zip(scores, values_bf):
        p = jnp.exp(s - m_new)
        l = l + jnp.sum(p, axis=-1, keepdims=True)
        acc = acc + _dot(_bf(p), v_bf)
    l_sc[...] = l
    acc_sc[...] = acc
    m_sc[...] = m_new


def _fox_decode_kernel(pt_ref, q_ref, kn_ref, vn_ref, lfr_ref, lfc_ref, *rest, P, T):
    k_refs, v_refs, w_refs = rest[:P], rest[P:2 * P], rest[2 * P:3 * P]
    o_ref, m_sc, l_sc, acc_sc, carry_sc = rest[3 * P:]
    s_id = pl.program_id(1)
    R = FOX_HEADS * T
    scale = FOX_HEAD_DIM ** -0.5
    qb = _bf(q_ref[0])

    @pl.when(s_id == 0)
    def _():
        r = lax.broadcasted_iota(jnp.int32, (R, R), 0)
        c = lax.broadcasted_iota(jnp.int32, (R, R), 1)
        pre = ((r % FOX_HEADS) == (c % FOX_HEADS)) & (r <= c)
        cn = jnp.sum(jnp.where(pre, lfc_ref[0], 0.0), axis=0, keepdims=True)
        s = _dot_nt(qb, _bf(kn_ref[0])) * scale - cn
        ok = ((c % FOX_HEADS) == (r // T)) & ((c // FOX_HEADS) <= (r % T))
        s = jnp.where(ok, s, NEG_BIG)
        m = jnp.max(s, axis=-1, keepdims=True)
        p = jnp.exp(s - m)
        m_sc[...] = m
        l_sc[...] = jnp.sum(p, axis=-1, keepdims=True)
        acc_sc[...] = _dot(_bf(p), _bf(vn_ref[0]))
        carry_sc[...] = jnp.zeros_like(carry_sc)

    n = FOX_PAGE_COLS
    r = lax.broadcasted_iota(jnp.int32, (R, n), 0)
    c = lax.broadcasted_iota(jnp.int32, (R, n), 1)
    ok = (c % FOX_HEADS) == (r // T)
    carry = carry_sc[...]
    scores = []
    for i in range(P):
        w = w_refs[i][0]
        kk = _bf(k_refs[i][0].reshape(n, FOX_HEAD_DIM))
        s = _dot_nt(qb, kk) * scale + (w[:, :n] + carry)
        scores.append(jnp.where(ok, s, NEG_BIG))
        carry = carry + w[:, n:]
    carry_sc[...] = carry
    _online_update_many(scores, [_bf(v_refs[i][0].reshape(n, FOX_HEAD_DIM)) for i in range(P)],
                        m_sc, l_sc, acc_sc)

    @pl.when(s_id == pl.num_programs(1) - 1)
    def _():
        o_ref[0] = acc_sc[...] / l_sc[...]


def fox_decode(page_table, q, k_new, v_new, lf_row, lf_col, pool_k, pool_v, pool_w, page_off, P):
    DB, R, dh = q.shape
    n_pages = page_table.shape[1]
    T = R // FOX_HEADS
    assert n_pages % P == 0
    steps = n_pages // P
    n = FOX_PAGE_COLS
    page = pool_k.shape[1]
    assert page * FOX_HEADS == n

    def page_map(i, nd):
        return lambda b, s, pt: (page_off + pt[b, n_pages - 1 - (s * P + i)],) + (0,) * nd

    per_b = lambda shape: pl.BlockSpec((1,) + shape, lambda b, s, pt: (b, 0, 0))
    in_specs = [per_b((R, dh)), per_b((R, dh)), per_b((R, dh)), per_b((1, R)), per_b((R, 1))]
    in_specs += [pl.BlockSpec((1, page, FOX_HEADS, dh), page_map(i, 3)) for i in range(P)]
    in_specs += [pl.BlockSpec((1, page, FOX_HEADS, dh), page_map(i, 3)) for i in range(P)]
    in_specs += [pl.BlockSpec((1, 1, 2 * n), page_map(i, 2)) for i in range(P)]
    grid_spec = pltpu.PrefetchScalarGridSpec(
        num_scalar_prefetch=1, grid=(DB, steps), in_specs=in_specs,
        out_specs=per_b((R, dh)),
        scratch_shapes=[pltpu.VMEM((R, 1), jnp.float32), pltpu.VMEM((R, 1), jnp.float32),
                        pltpu.VMEM((R, dh), jnp.float32), pltpu.VMEM((1, n), jnp.float32)])
    return pl.pallas_call(
        functools.partial(_fox_decode_kernel, P=P, T=T),
        grid_spec=grid_spec,
        out_shape=jax.ShapeDtypeStruct((DB, R, dh), jnp.float32),
        compiler_params=_cparams(("arbitrary", "arbitrary")),
        name="fox_decode",
    )(page_table, q, k_new, v_new, lf_row, lf_col, *([pool_k] * P), *([pool_v] * P), *([pool_w] * P))


def _mla_decode_kernel(pt_ref, ql_ref, qr_ref, cn_ref, kn_ref, *rest, P, T):
    c_refs, k_refs = rest[:P], rest[P:2 * P]
    o_ref, m_sc, l_sc, acc_sc = rest[2 * P:]
    s_id = pl.program_id(1)
    R = MLA_HEADS * T
    ql = _bf(ql_ref[0])
    qr = _bf(qr_ref[0])

    @pl.when(s_id == 0)
    def _():
        cn = _bf(cn_ref[0])
        s = (_dot_nt(ql, cn) + _dot_nt(qr, _bf(kn_ref[0]))) * MLA_SCALE
        r = lax.broadcasted_iota(jnp.int32, (R, T), 0)
        c = lax.broadcasted_iota(jnp.int32, (R, T), 1)
        s = jnp.where(c <= (r % T), s, NEG_BIG)
        m = jnp.max(s, axis=-1, keepdims=True)
        p = jnp.exp(s - m)
        m_sc[...] = m
        l_sc[...] = jnp.sum(p, axis=-1, keepdims=True)
        acc_sc[...] = _dot(_bf(p), cn)

    pages = [_bf(c_refs[i][0]) for i in range(P)]
    scores = [(_dot_nt(ql, pages[i]) + _dot_nt(qr, _bf(k_refs[i][0]))) * MLA_SCALE for i in range(P)]
    _online_update_many(scores, pages, m_sc, l_sc, acc_sc)

    @pl.when(s_id == pl.num_programs(1) - 1)
    def _():
        o_ref[0] = acc_sc[...] / l_sc[...]


def mla_decode(page_table, q_lat, q_rope, ckv_new, kr_new, pool_ckv, pool_kr, page_off, P):
    DB, R, C = q_lat.shape
    n_pages = page_table.shape[1]
    T = R // MLA_HEADS
    steps = n_pages // P
    ps, rr = pool_ckv.shape[1], pool_kr.shape[2]

    def page_map(i):
        return lambda b, s, pt: (page_off + pt[b, s * P + i], 0, 0)

    per_b = lambda shape: pl.BlockSpec((1,) + shape, lambda b, s, pt: (b, 0, 0))
    in_specs = [per_b((R, C)), per_b((R, rr)), per_b((T, C)), per_b((T, rr))]
    in_specs += [pl.BlockSpec((1, ps, C), page_map(i)) for i in range(P)]
    in_specs += [pl.BlockSpec((1, ps, rr), page_map(i)) for i in range(P)]
    grid_spec = pltpu.PrefetchScalarGridSpec(
        num_scalar_prefetch=1, grid=(DB, steps), in_specs=in_specs,
        out_specs=per_b((R, C)),
        scratch_shapes=[pltpu.VMEM((R, 1), jnp.float32), pltpu.VMEM((R, 1), jnp.float32),
                        pltpu.VMEM((R, C), jnp.float32)])
    return pl.pallas_call(
        functools.partial(_mla_decode_kernel, P=P, T=T),
        grid_spec=grid_spec,
        out_shape=jax.ShapeDtypeStruct((DB, R, C), jnp.float32),
        compiler_params=_cparams(("arbitrary", "arbitrary")),
        name="mla_decode",
    )(page_table, q_lat, q_rope, ckv_new, kr_new, *([pool_ckv] * P), *([pool_kr] * P))


def _top_rows(vals, k):
    n = vals.shape[0]
    pos = lax.broadcasted_iota(jnp.int32, vals.shape, 0).astype(jnp.float32)
    top_v, top_p = [], []
    for _ in range(k):
        m = jnp.max(vals, axis=0, keepdims=True)
        at = jnp.min(jnp.where(vals == m, pos, float(n)), axis=0, keepdims=True)
        top_v.append(m)
        top_p.append(at)
        vals = jnp.where(pos == at, -jnp.inf, vals)
    return jnp.concatenate(top_v, axis=0), jnp.concatenate(top_p, axis=0)


def _pick_rows(table, at):
    out = jnp.zeros(at.shape, table.dtype)
    for a in range(table.shape[0]):
        out = jnp.where(at == float(a), table[a:a + 1, :], out)
    return out


def _peer_select_kernel(x_ref, wh_ref, wl_ref, skh_ref, skl_ref, idx_ref, g_ref):
    x = x_ref[...]
    xh = _bf(x)
    xl = _bf(x - xh.astype(jnp.float32))
    wh = wh_ref[...]
    qt = _dot_nt(wh, xh) + _dot_nt(wh, xl) + _dot_nt(wl_ref[...], xh)
    half = PEER_KEY_DIM // 2
    K = PEER_TOPK
    for h in range(PEER_HEADS):
        tops = []
        for p in range(2):
            j = 2 * h + p
            q = qt[j * half:(j + 1) * half, :]
            qh = _bf(q)
            ql = _bf(q - qh.astype(jnp.float32))
            skh = skh_ref[j]
            s = _dot(skh, qh) + _dot(skh, ql) + _dot(skl_ref[j], qh)
            tops.append(_top_rows(s, K))
        (s1, i1), (s2, i2) = tops
        cand_s = jnp.concatenate([s1[a:a + 1, :] + s2 for a in range(K)], axis=0)
        best_s, best_j = _top_rows(cand_s, K)
        ja = jnp.floor(best_j * (1.0 / K))
        jb = best_j - ja * K
        best_i = _pick_rows(i1, ja) * PEER_N_KEYS + _pick_rows(i2, jb)
        e = jnp.exp(best_s - best_s[0:1, :])
        g_ref[0, h * K:(h + 1) * K, :] = e / jnp.sum(e, axis=0, keepdims=True)
        idx_ref[0, h * K:(h + 1) * K, :] = best_i.astype(jnp.int32)


def peer_select(xn, wq_t_hi, wq_t_lo, sk_hi, sk_lo, tm):
    N, D = xn.shape
    HK = PEER_HEADS * PEER_TOPK
    full = lambda a: pl.BlockSpec(a.shape, lambda i: (0,) * a.ndim)
    return pl.pallas_call(
        _peer_select_kernel,
        grid=(N // tm,),
        in_specs=[pl.BlockSpec((tm, D), lambda i: (i, 0)),
                  full(wq_t_hi), full(wq_t_lo), full(sk_hi), full(sk_lo)],
        out_specs=[pl.BlockSpec((1, HK, tm), lambda i: (i, 0, 0)),
                   pl.BlockSpec((1, HK, tm), lambda i: (i, 0, 0))],
        out_shape=[jax.ShapeDtypeStruct((N // tm, HK, tm), jnp.int32),
                   jax.ShapeDtypeStruct((N // tm, HK, tm), jnp.float32)],
        compiler_params=_cparams(("arbitrary",)),
        name="peer_select",
    )(xn, wq_t_hi, wq_t_lo, sk_hi, sk_lo)


PEER_ROWS = PEER_HEADS * PEER_TOPK
PK_SUB = 4


def _unpack_row(w):
    hi = lax.bitcast_convert_type(w & jnp.uint32(0xFFFF0000), jnp.float32)
    lo = lax.bitcast_convert_type(w << 16, jnp.float32)
    return hi, lo


_FOLD_SLOT_ROW = (3, 7, 1, 5, 2, 6, 0, 4)


def _fold8(tiles):
    sub = lax.broadcasted_iota(jnp.int32, (8, LANES), 0)
    pairs = [jnp.concatenate([tiles[2 * i], tiles[2 * i + 1]], axis=0) for i in range(4)]
    t = [c + pltpu.roll(c, 2, 0) for c in pairs]
    keep2 = (sub & 2) != 0
    u = [jnp.where(keep2, t[0], pltpu.roll(t[1], 6, 0)), jnp.where(keep2, t[2], pltpu.roll(t[3], 6, 0))]
    v = [w + pltpu.roll(w, 1, 0) for w in u]
    return jnp.where((sub & 1) != 0, v[0], pltpu.roll(v[1], 7, 0))


def _peer_act_kernel(idx_ref, x_ref, tab_ref, o_ref, *, tb):
    lane = lax.broadcasted_iota(jnp.int32, (8, LANES), 1)

    def token(t, carry):
        xh = x_ref[t, 0:PK_SUB, :]
        xl = x_ref[t, PK_SUB:2 * PK_SUB, :]
        tile = jnp.zeros((8, LANES), jnp.float32)
        for gi in range(PEER_ROWS // 8):
            prods = []
            for s in _FOLD_SLOT_ROW:
                hi, lo = _unpack_row(tab_ref[idx_ref[t, gi * 8 + s]])
                prods.append(hi * xh + lo * xl)
            grp = _fold8(prods)
            tile = jnp.where(lane == gi, jnp.sum(grp, axis=1, keepdims=True), tile)
        o_ref[t] = tile
        return carry

    lax.fori_loop(0, tb, token, 0)


def peer_act(idx, x8, u_pk, tb):
    N = idx.shape[0]
    return pl.pallas_call(
        functools.partial(_peer_act_kernel, tb=tb),
        grid=(N // tb,),
        in_specs=[pl.BlockSpec((tb, PEER_ROWS), lambda i: (i, 0), memory_space=pltpu.SMEM),
                  pl.BlockSpec((tb, 8, LANES), lambda i: (i, 0, 0)),
                  pl.BlockSpec(u_pk.shape, lambda i: (0, 0, 0), pipeline_mode=pl.Buffered(1))],
        out_specs=pl.BlockSpec((tb, 8, LANES), lambda i: (i, 0, 0)),
        out_shape=jax.ShapeDtypeStruct((N, 8, LANES), jnp.float32),
        compiler_params=pltpu.CompilerParams(dimension_semantics=("arbitrary",),
                                             vmem_limit_bytes=_PEER_VMEM_LIMIT),
        name="peer_act",
    )(idx, x8, u_pk)


def _peer_mix_kernel(idx_ref, coef_ref, tab_ref, o_ref, *, tb):
    def token(t, carry):
        acc_h = jnp.zeros((PK_SUB, LANES), jnp.float32)
        acc_l = jnp.zeros((PK_SUB, LANES), jnp.float32)
        for r in range(PEER_ROWS):
            hi, lo = _unpack_row(tab_ref[idx_ref[t, r]])
            c = coef_ref[t, r]
            acc_h = acc_h + c * hi
            acc_l = acc_l + c * lo
        o_ref[t] = jnp.concatenate([acc_h, acc_l], axis=0)
        return carry

    lax.fori_loop(0, tb, token, 0)


def peer_mix(idx, coef, v_pk, tb):
    N = idx.shape[0]
    return pl.pallas_call(
        functools.partial(_peer_mix_kernel, tb=tb),
        grid=(N // tb,),
        in_specs=[pl.BlockSpec((tb, PEER_ROWS), lambda i: (i, 0), memory_space=pltpu.SMEM),
                  pl.BlockSpec((tb, PEER_ROWS), lambda i: (i, 0), memory_space=pltpu.SMEM),
                  pl.BlockSpec(v_pk.shape, lambda i: (0, 0, 0), pipeline_mode=pl.Buffered(1))],
        out_specs=pl.BlockSpec((tb, 8, LANES), lambda i: (i, 0, 0)),
        out_shape=jax.ShapeDtypeStruct((N, 8, LANES), jnp.float32),
        compiler_params=pltpu.CompilerParams(dimension_semantics=("arbitrary",),
                                             vmem_limit_bytes=_PEER_VMEM_LIMIT),
        name="peer_mix",
    )(idx, coef, v_pk)


def _peer_coef_kernel(a_ref, g_ref, c_ref):
    a = a_ref[...]
    gelu = 0.5 * a * (1.0 + jnp.tanh(math.sqrt(2.0 / math.pi) * (a + 0.044715 * a * a * a)))
    c_ref[...] = g_ref[...] * gelu


def peer_coef(act, g, tm):
    N, Rr = act.shape
    spec = pl.BlockSpec((tm, Rr), lambda i: (i, 0))
    return pl.pallas_call(
        _peer_coef_kernel, grid=(N // tm,), in_specs=[spec, spec], out_specs=spec,
        out_shape=jax.ShapeDtypeStruct((N, Rr), jnp.float32),
        compiler_params=_cparams(("arbitrary",)), name="peer_coef",
    )(act, g)


def _residual_kernel(x_ref, gate_ref, y_ref, g_ref, o_ref, *, final_norm):
    x = x_ref[...] + gate_ref[0] * y_ref[...]
    if final_norm:
        x = _rms(x, g_ref[...])
    o_ref[...] = x


def residual_out(x, gate, y, g_final, rows_per_group, tm, final_norm):
    N, D = x.shape
    G, R, _ = gate.shape
    if R == 1:
        bpg = rows_per_group // tm
        mod_spec = pl.BlockSpec((1, 1, D), lambda i: (i // bpg, 0, 0))
    else:
        mod_spec = pl.BlockSpec((1, tm, D), lambda i: (0, i, 0))
    spec = pl.BlockSpec((tm, D), lambda i: (i, 0))
    return pl.pallas_call(
        functools.partial(_residual_kernel, final_norm=final_norm),
        grid=(N // tm,),
        in_specs=[spec, mod_spec, spec, pl.BlockSpec((1, D), lambda i: (0, 0))],
        out_specs=spec,
        out_shape=jax.ShapeDtypeStruct((N, D), jnp.float32),
        compiler_params=_cparams(("arbitrary",)), name="residual_out",
    )(x, gate, y, g_final.reshape(1, D))


def pack_expert_rows(w):
    E, D = w.shape
    b = lax.bitcast_convert_type(w.astype(jnp.bfloat16), jnp.uint16).astype(jnp.uint32)
    pk = (b[:, :D // 2] << 16) | b[:, D // 2:]
    return pk.reshape(E, PK_SUB, LANES)


def rope_tables(pos):
    half = MLA_ROPE // 2
    freqs = ROPE_THETA ** (-jnp.arange(half, dtype=jnp.float32) / half)
    ang = pos.astype(jnp.float32)[:, None] * freqs[None, :]
    cos, sin = jnp.cos(ang), jnp.sin(ang)
    return jnp.tile(cos, (1, 8)), jnp.concatenate([-sin] * 4 + [sin] * 4, axis=1)


def mla_weights(w_uq, w_uk, w_uv):
    H, NP, hr = MLA_HEADS, MLA_NOPE, MLA_ROPE // 2
    nope = w_uq[:, :, :NP].reshape(MLA_Q_LORA, H * NP)
    x1 = w_uq[:, :, NP:NP + hr].reshape(MLA_Q_LORA, H * hr)
    x2 = w_uq[:, :, NP + hr:].reshape(MLA_Q_LORA, H * hr)
    w_uq_r = jnp.concatenate([nope, x1, x2], axis=1).astype(jnp.bfloat16)
    w_uk_bd = jnp.zeros((H * NP, H * MLA_KV_LORA), jnp.float32)
    w_uv_bd = jnp.zeros((H * MLA_KV_LORA, H * MLA_V), jnp.float32)
    for h in range(H):
        w_uk_bd = w_uk_bd.at[h * NP:(h + 1) * NP, h * MLA_KV_LORA:(h + 1) * MLA_KV_LORA].set(w_uk[:, h, :].T)
        w_uv_bd = w_uv_bd.at[h * MLA_KV_LORA:(h + 1) * MLA_KV_LORA, h * MLA_V:(h + 1) * MLA_V].set(w_uv[:, h, :])
    return w_uq_r, w_uk_bd.astype(jnp.bfloat16), w_uv_bd.astype(jnp.bfloat16)


_IN_SIZES = (256, 256, 256, 4, 512, 512, 512, 512, 4, 4, 256, 128, 32, 3072)


def pad_w_in(w_in):
    D = w_in.shape[0]
    offs = [0]
    for s in _IN_SIZES:
        offs.append(offs[-1] + s)
    fq, fk, fv, ff, mq, mk, mv, mo, mi, mf, cq, ckv, ckr, zg = (
        w_in[:, offs[i]:offs[i + 1]] for i in range(len(_IN_SIZES)))
    hr = MLA_ROPE // 2
    ckr_t = jnp.concatenate([ckr[:, :hr]] * 4 + [ckr[:, hr:]] * 4, axis=1)
    small = jnp.concatenate([ff, mi, mf, jnp.zeros((D, LANES - 12), w_in.dtype)], axis=1)
    pad = jnp.zeros((D, LANES), w_in.dtype)
    w = jnp.concatenate([zg, mq, mk, mv, mo, fq, fk, fv, cq, ckv, ckr_t, small, pad], axis=1)
    return w.astype(jnp.bfloat16)


def small_bias_row(b_fox_f, b_mlstm_i, b_mlstm_f):
    return jnp.concatenate([b_fox_f, b_mlstm_i, b_mlstm_f,
                            jnp.zeros((LANES - 12,), jnp.float32)]).reshape(1, LANES)


def _split2(w):
    h = w.astype(jnp.bfloat16)
    return h, (w - h.astype(jnp.float32)).astype(jnp.bfloat16)


def _krope_rows(krt):
    hr = MLA_ROPE // 2
    return jnp.concatenate([krt[:, :hr], krt[:, LANES // 2:LANES // 2 + hr]], axis=1)


def _token_major(a):
    nb, r, tm = a.shape
    return a.transpose(0, 2, 1).reshape(nb * tm, r)


PROMPT_TM = 512
ATTN_T_FOX = 512
ATTN_T_MLA = 512
SAMPLE_TM = 256
PEER_TB = 128
FOX_DECODE_PAGES = 8
MLA_DECODE_PAGES = 16
SAMPLE_CHUNK = 16


def _peer_ffn(xn2, lw, tm_sel):
    N, D = xn2.shape
    idx_t, g_t = peer_select(xn2, lw["wq_hi"], lw["wq_lo"], lw["sk_hi"], lw["sk_lo"], tm_sel)
    idx, g = _token_major(idx_t), _token_major(g_t)
    a8 = peer_act(idx, xn2.reshape(N, 8, LANES), lw["u_pk"], PEER_TB)
    act = a8[:, :, :PEER_ROWS // 8].transpose(0, 2, 1).reshape(N, PEER_ROWS)
    coef = peer_coef(act, g, tm_sel)
    return peer_mix(idx, coef, lw["v_pk"], PEER_TB).reshape(N, D)


def kernel(x_prompt, x_sample, c_prompt, c_sample, cache_fox_k, cache_fox_v, cache_fox_logf, cache_mla_ckv, cache_mla_krope, state_mlstm_C, state_mlstm_n, state_mlstm_m, page_table, w_ada, b_ada, g_norm_mix, g_norm_ffn, w_in, b_fox_f, b_mlstm_i, b_mlstm_f, g_mla_q, g_mla_kv, w_mla_uq, w_mla_uk, w_mla_uv, w_branch_fox, w_branch_mlstm, w_branch_mla, w_out, w_peer_q, peer_subkeys, peer_u, peer_v, g_final):
    B, S, D = x_prompt.shape
    DB, T, _ = x_sample.shape
    depth = w_in.shape[0]
    n_pool, page = cache_fox_k.shape[1], cache_fox_k.shape[2]
    n_pages = page_table.shape[1]
    past = n_pages * page
    Np, Ns = B * S, DB * T
    H = FOX_HEADS
    LP = SAMPLE_CHUNK

    xp = x_prompt.reshape(Np, D)
    xs = x_sample.reshape(Ns, D)
    n_c = B + DB
    c_all = jnp.concatenate([c_prompt, c_sample, jnp.zeros((-n_c % 8, D), jnp.float32)], axis=0)

    pool_k = cache_fox_k.reshape(depth * n_pool, page, H, FOX_HEAD_DIM)
    pool_v = cache_fox_v.reshape(depth * n_pool, page, H, FOX_HEAD_DIM)
    pool_w = pool_logf_suffix(cache_fox_logf.reshape(depth * n_pool, page * H), 512)
    pool_w = pool_w.reshape(depth * n_pool, 1, 2 * FOX_PAGE_COLS)
    pool_ckv = cache_mla_ckv.reshape(depth * n_pool, page, MLA_KV_LORA)
    pool_kr = cache_mla_krope.reshape(depth * n_pool, page, MLA_ROPE)
    cos_p, sin_p = rope_tables(jnp.arange(S))
    cos_s, sin_s = (jnp.tile(a, (DB, 1)) for a in rope_tables(past + jnp.arange(T)))

    prompt_rows, sample_rows = [], []
    for l in range(depth):
        w_uq_r, w_uk_bd, w_uv_bd = mla_weights(w_mla_uq[l], w_mla_uk[l], w_mla_uv[l])
        wq_hi, wq_lo = _split2(w_peer_q[l].T)
        sk_hi, sk_lo = _split2(peer_subkeys[l].reshape(2 * PEER_HEADS, PEER_N_KEYS, PEER_KEY_DIM // 2))
        lw = dict(wq_hi=wq_hi, wq_lo=wq_lo, sk_hi=sk_hi, sk_lo=sk_lo,
                  u_pk=pack_expert_rows(peer_u[l]), v_pk=pack_expert_rows(peer_v[l]))
        w_bf = pad_w_in(w_in[l])
        bias_row = small_bias_row(b_fox_f[l], b_mlstm_i[l], b_mlstm_f[l])
        wbf, wbm, wbc, wout = (_bf(w_branch_fox[l]), _bf(w_branch_mlstm[l]),
                               _bf(w_branch_mla[l]), _bf(w_out[l]))
        last = l == depth - 1

        mods = ada_ln(c_all, w_ada[l], b_ada[l])
        mp = mods[:B].reshape(B, 6, D)
        ms = mods[B:n_c].reshape(DB, 6, D)
        mod_p = [mp[:, i][:, None, :] for i in range(6)]
        mod_s = [jnp.repeat(ms[:, i], T, axis=0)[None] for i in range(6)]

        z = in_proj(xp, mod_p[0], mod_p[1], g_norm_mix[l], w_bf, S, PROMPT_TM)
        sm, cum = small_gates(z, bias_row, B, S, 256)
        cum_t = jnp.pad(jnp.swapaxes(cum[:, :H].reshape(B, S, H), 1, 2), ((0, 0), (0, 8 - H), (0, 0)))
        o_fox = fox_prompt(z, cum, cum_t, B, S, ATTN_T_FOX)
        qc, kc, ckvn, krt, _ = mla_prep(z, cos_p, sin_p, S // PROMPT_TM, g_mla_q[l], w_uq_r, w_uk_bd,
                                        g_mla_kv[l], PROMPT_TM)
        o_lat = mla_prompt(qc, kc, B, S, ATTN_T_MLA)
        nc = S // MLSTM_CHUNK
        hw = MLSTM_HEADS * MLSTM_DK
        kt = z[:, MK_OFF:MK_OFF + hw].reshape(B, S, MLSTM_HEADS, MLSTM_DK).transpose(0, 2, 3, 1)
        smt = sm[:, :16].reshape(B * nc, MLSTM_CHUNK, 16).transpose(0, 2, 1)
        h_ml, C_p, n_p, m_p = mlstm_scan(z, z, z, (MQ_OFF // hw, MK_OFF // hw, MV_OFF // hw), kt, sm, smt,
                                         None, B, nc, MLSTM_CHUNK, MLSTM_CHUNK)
        x1, xn2 = merge_block(xp, mod_p[2], mod_p[3], mod_p[4], o_fox, h_ml, z, o_lat,
                              wbf, wbm, w_uv_bd, wbc, wout, g_norm_ffn[l], S, 256)
        y = _peer_ffn(xn2, lw, 256)
        xp = residual_out(x1, mod_p[5], y, g_final, S, PROMPT_TM, last)
        prompt_rows.append((
            z[:, FK_OFF:FK_OFF + 256].reshape(B, S, H, FOX_HEAD_DIM),
            z[:, FV_OFF:FV_OFF + 256].reshape(B, S, H, FOX_HEAD_DIM),
            sm[:, :H].reshape(B, S, H),
            ckvn.reshape(B, S, MLA_KV_LORA),
            _krope_rows(krt).reshape(B, S, MLA_ROPE),
            C_p, n_p[:, :, 0, :], m_p[:, :, 0, 0]))

        zs = in_proj(xs, mod_s[0], mod_s[1], g_norm_mix[l], w_bf, Ns, SAMPLE_TM)
        sm_s, _ = small_gates(zs, bias_row, 1, Ns, SAMPLE_TM)
        fq = zs[:, FQ_OFF:FQ_OFF + 256].reshape(DB, T, H, FOX_HEAD_DIM).transpose(0, 2, 1, 3)
        fk_s = zs[:, FK_OFF:FK_OFF + 256]
        fv_s = zs[:, FV_OFF:FV_OFF + 256]
        lf_s = sm_s[:, :H]
        o_f = fox_decode(page_table, fq.reshape(DB, H * T, FOX_HEAD_DIM),
                         fk_s.reshape(DB, T * H, FOX_HEAD_DIM), fv_s.reshape(DB, T * H, FOX_HEAD_DIM),
                         lf_s.reshape(DB, 1, T * H), lf_s.reshape(DB, T * H, 1),
                         pool_k, pool_v, pool_w, l * n_pool, FOX_DECODE_PAGES)
        o_fox_s = o_f.reshape(DB, H, T, FOX_HEAD_DIM).transpose(0, 2, 1, 3).reshape(Ns, H * FOX_HEAD_DIM)
        qc_s, _, ckvn_s, krt_s, qrt_s = mla_prep(zs, cos_s, sin_s, Ns // SAMPLE_TM, g_mla_q[l], w_uq_r,
                                                 w_uk_bd, g_mla_kv[l], SAMPLE_TM)
        q_lat_s = qc_s[:, :, :MLA_KV_LORA].reshape(MLA_HEADS, DB, T, MLA_KV_LORA).transpose(1, 0, 2, 3)
        hr = MLA_ROPE // 2
        q_rope_s = qrt_s.reshape(DB, T, 2, MLA_HEADS, hr).transpose(0, 3, 1, 2, 4)
        kr_s = _krope_rows(krt_s)
        o_l = mla_decode(page_table, q_lat_s.reshape(DB, MLA_HEADS * T, MLA_KV_LORA),
                         q_rope_s.reshape(DB, MLA_HEADS * T, MLA_ROPE),
                         ckvn_s.reshape(DB, T, MLA_KV_LORA), kr_s.reshape(DB, T, MLA_ROPE),
                         pool_ckv, pool_kr, l * n_pool, MLA_DECODE_PAGES)
        o_lat_s = o_l.reshape(DB, MLA_HEADS, T, MLA_KV_LORA).transpose(1, 0, 2, 3).reshape(MLA_HEADS, Ns, MLA_KV_LORA)

        def pad_rows(a):
            a = a.reshape(DB, T, a.shape[-1])
            return jnp.pad(a, ((0, 0), (0, LP - T), (0, 0))).reshape(DB * LP, a.shape[-1])

        mq_p = pad_rows(zs[:, MQ_OFF:MQ_OFF + hw])
        mk_p = pad_rows(zs[:, MK_OFF:MK_OFF + hw])
        mv_p = pad_rows(zs[:, MV_OFF:MV_OFF + hw])
        sm_p = pad_rows(sm_s)
        kt_s = mk_p.reshape(DB, LP, MLSTM_HEADS, MLSTM_DK).transpose(0, 2, 3, 1)
        smt_s = sm_p[:, :16].reshape(DB, LP, 16).transpose(0, 2, 1)
        state = (state_mlstm_C[l], state_mlstm_n[l][:, :, None, :],
                 jnp.broadcast_to(state_mlstm_m[l][:, :, None, None], (DB, MLSTM_HEADS, 1, LANES)))
        h_s, C_s, n_s, m_s = mlstm_scan(mq_p, mk_p, mv_p, (0, 0, 0), kt_s, sm_p, smt_s, state, DB, 1, LP, T)
        h_ml_s = h_s.reshape(DB, LP, hw)[:, :T].reshape(Ns, hw)
        x1s, xn2s = merge_block(xs, mod_s[2], mod_s[3], mod_s[4], o_fox_s, h_ml_s, zs, o_lat_s,
                                wbf, wbm, w_uv_bd, wbc, wout, g_norm_ffn[l], Ns, SAMPLE_TM)
        ys = _peer_ffn(xn2s, lw, SAMPLE_TM)
        xs = residual_out(x1s, mod_s[5], ys, g_final, Ns, SAMPLE_TM, last)
        sample_rows.append((
            fk_s.reshape(DB, T, H, FOX_HEAD_DIM), fv_s.reshape(DB, T, H, FOX_HEAD_DIM),
            lf_s.reshape(DB, T, H), ckvn_s.reshape(DB, T, MLA_KV_LORA), kr_s.reshape(DB, T, MLA_ROPE),
            C_s, n_s[:, :, 0, :], m_s[:, :, 0, 0]))

    p_out = [jnp.stack(r) for r in zip(*prompt_rows)]
    s_out = [jnp.stack(r) for r in zip(*sample_rows)]
    return (xp.reshape(B, S, D), xs.reshape(DB, T, D), *p_out, *s_out)
```

```python
import functools
import math

import jax
import jax.numpy as jnp
from jax import lax
from jax.experimental import pallas as pl
from jax.experimental.pallas import tpu as pltpu

D_MODEL = 1024
FOX_HEADS = 4
FOX_HEAD_DIM = 64
MLSTM_HEADS = 4
MLSTM_DK = 128
MLSTM_DV = 128
MLSTM_CHUNK = 128
MLA_HEADS = 4
MLA_Q_LORA = 256
MLA_KV_LORA = 128
MLA_NOPE = 64
MLA_ROPE = 32
MLA_V = 64
MLA_SCALE = (MLA_NOPE + MLA_ROPE) ** -0.5
ROPE_THETA = 10000.0
PEER_HEADS = 8
PEER_N_KEYS = 128
PEER_KEY_DIM = 128
PEER_TOPK = 16
NORM_EPS = 1e-6
N_BRANCH = 3

LANES = 128
NEG_BIG = -1e30

ZG_OFF = 0
MQ_OFF = 3072
MK_OFF = MQ_OFF + 512
MV_OFF = MK_OFF + 512
MO_OFF = MV_OFF + 512
FQ_OFF = 5120
FK_OFF = FQ_OFF + 256
FV_OFF = FK_OFF + 256
CQ_OFF = 5888
CKV_OFF = 6144
CKR_OFF = 6272
SM_OFF = 6400
Z_COLS = 6656
SM_FOXF = 0
SM_MI = 4
SM_MF = 8

_VMEM_LIMIT = 48 * 1024 * 1024
_PEER_VMEM_LIMIT = 44 * 1024 * 1024


def _cparams(sem):
    return pltpu.CompilerParams(dimension_semantics=sem, vmem_limit_bytes=_VMEM_LIMIT)


def _bf(x):
    return x.astype(jnp.bfloat16)


def _dot(a, b):
    return jnp.dot(a, b, preferred_element_type=jnp.float32)


def _dot_nt(a, b):
    return lax.dot_general(a, b, (((1,), (1,)), ((), ())), preferred_element_type=jnp.float32)


def _split3(x):
    h = _bf(x)
    r = x - h.astype(jnp.float32)
    m = _bf(r)
    l = _bf(r - m.astype(jnp.float32))
    return h, m, l


def _log_sigmoid(x):
    return jnp.minimum(x, 0.0) - jnp.log1p(jnp.exp(-jnp.abs(x)))


def _ada_kernel(c_ref, w_ref, b_ref, o_ref):
    c = c_ref[...]
    a = c * (1.0 / (1.0 + jnp.exp(-c)))
    o_ref[...] = _dot(_bf(a), _bf(w_ref[...])) + b_ref[...]


def ada_ln(c, w, b):
    R, D = c.shape
    N = w.shape[1]
    tn = 1024
    return pl.pallas_call(
        _ada_kernel,
        grid=(N // tn,),
        in_specs=[pl.BlockSpec((R, D), lambda j: (0, 0)),
                  pl.BlockSpec((D, tn), lambda j: (0, j)),
                  pl.BlockSpec((1, tn), lambda j: (0, j))],
        out_specs=pl.BlockSpec((R, tn), lambda j: (0, j)),
        out_shape=jax.ShapeDtypeStruct((R, N), jnp.float32),
        compiler_params=_cparams(("arbitrary",)),
        name="ada_ln",
    )(c, w, b.reshape(1, N))


def _modulate(x, g, shift, scale):
    y = x * lax.rsqrt(jnp.mean(x * x, axis=-1, keepdims=True) + NORM_EPS)
    return (y * g) * (1.0 + scale) + shift


def _inproj_kernel(x_ref, sh_ref, sc_ref, g_ref, w_ref, z_ref, xn_sc):
    @pl.when(pl.program_id(1) == 0)
    def _():
        xn_sc[...] = _bf(_modulate(x_ref[...], g_ref[...], sh_ref[0], sc_ref[0]))

    z_ref[...] = _dot(xn_sc[...], w_ref[...])


def in_proj(x, shift, scale, g, w_bf, rows_per_group, tm):
    N, D = x.shape
    G, R, _ = shift.shape
    tn = 512
    if R == 1:
        bpg = rows_per_group // tm
        mod_spec = pl.BlockSpec((1, 1, D), lambda i, j: (i // bpg, 0, 0))
    else:
        mod_spec = pl.BlockSpec((1, tm, D), lambda i, j: (0, i, 0))
    return pl.pallas_call(
        _inproj_kernel,
        grid=(N // tm, Z_COLS // tn),
        in_specs=[pl.BlockSpec((tm, D), lambda i, j: (i, 0)),
                  mod_spec, mod_spec,
                  pl.BlockSpec((1, D), lambda i, j: (0, 0)),
                  pl.BlockSpec((D, tn), lambda i, j: (0, j))],
        out_specs=pl.BlockSpec((tm, tn), lambda i, j: (i, j)),
        out_shape=jax.ShapeDtypeStruct((N, Z_COLS), jnp.float32),
        scratch_shapes=[pltpu.VMEM((tm, D), jnp.bfloat16)],
        compiler_params=_cparams(("arbitrary", "arbitrary")),
        name="in_proj",
    )(x, shift, scale, g.reshape(1, D), w_bf)


def _small_kernel(z_ref, bias_ref, sm_ref, cum_ref, carry_sc, *, tb):
    @pl.when(pl.program_id(1) == 0)
    def _():
        carry_sc[...] = jnp.zeros_like(carry_sc)

    v = z_ref[...] + bias_ref[...]
    lane = lax.broadcasted_iota(jnp.int32, v.shape, 1)
    is_ls = (lane < SM_MI) | ((lane >= SM_MF) & (lane < SM_MF + MLSTM_HEADS))
    sm = jnp.where(is_ls, _log_sigmoid(v), v)
    sm_ref[...] = sm
    row = lax.broadcasted_iota(jnp.int32, (tb, tb), 0)
    col = lax.broadcasted_iota(jnp.int32, (tb, tb), 1)
    tri = _bf(jnp.where(row >= col, 1.0, 0.0))
    h, m, l = _split3(sm)
    cum = _dot(tri, h) + _dot(tri, m) + _dot(tri, l) + carry_sc[...]
    cum_ref[...] = cum
    carry_sc[...] = cum[tb - 1:tb, :]


def small_gates(z, bias_row, n_groups, rows_per_group, tb):
    N = z.shape[0]
    nb = rows_per_group // tb
    cb = SM_OFF // LANES
    return pl.pallas_call(
        functools.partial(_small_kernel, tb=tb),
        grid=(n_groups, nb),
        in_specs=[pl.BlockSpec((tb, LANES), lambda b, i: (b * nb + i, cb)),
                  pl.BlockSpec((1, LANES), lambda b, i: (0, 0))],
        out_specs=[pl.BlockSpec((tb, LANES), lambda b, i: (b * nb + i, 0)),
                   pl.BlockSpec((tb, LANES), lambda b, i: (b * nb + i, 0))],
        out_shape=[jax.ShapeDtypeStruct((N, LANES), jnp.float32),
                   jax.ShapeDtypeStruct((N, LANES), jnp.float32)],
        scratch_shapes=[pltpu.VMEM((1, LANES), jnp.float32)],
        compiler_params=_cparams(("arbitrary", "arbitrary")),
        name="small_gates",
    )(z, bias_row)


def _fox_prompt_kernel(q_ref, k_ref, v_ref, fq_ref, fk_ref, o_ref, m_sc, l_sc, acc_sc, *, t):
    qi = pl.program_id(1)
    ki = pl.program_id(2)

    @pl.when(ki == 0)
    def _():
        m_sc[...] = jnp.full_like(m_sc, NEG_BIG)
        l_sc[...] = jnp.zeros_like(l_sc)
        acc_sc[...] = jnp.zeros_like(acc_sc)

    def step(diagonal):
        scale = FOX_HEAD_DIM ** -0.5
        fq = fq_ref[...]
        fk = fk_ref[0]
        half = lax.broadcasted_iota(jnp.int32, (1, LANES), 1) // FOX_HEAD_DIM
        for h in range(FOX_HEADS):
            sl = slice((h // 2) * LANES, (h // 2 + 1) * LANES)
            mine = half == (h % 2)
            qm = jnp.where(mine, q_ref[:, sl], 0.0)
            s = _dot_nt(_bf(qm), _bf(k_ref[:, sl])) * scale
            s = s + fq[:, h:h + 1] - fk[h:h + 1, :]
            if diagonal:
                row = lax.broadcasted_iota(jnp.int32, (t, t), 0)
                col = lax.broadcasted_iota(jnp.int32, (t, t), 1)
                s = jnp.where(row >= col, s, NEG_BIG)
            m_prev = m_sc[h]
            m_new = jnp.maximum(m_prev, jnp.max(s, axis=-1, keepdims=True))
            p = jnp.exp(s - jnp.tile(m_new, (1, t // LANES)))
            alpha = jnp.exp(m_prev - m_new)
            l_sc[h] = alpha * l_sc[h] + jnp.sum(p, axis=-1, keepdims=True)
            pv = _dot(_bf(p), _bf(v_ref[:, sl]))
            acc = acc_sc[h // 2]
            acc_sc[h // 2] = jnp.where(mine, alpha * acc + pv, acc)
            m_sc[h] = m_new

    @pl.when(ki < qi)
    def _():
        step(False)

    @pl.when(ki == qi)
    def _():
        step(True)
        half = lax.broadcasted_iota(jnp.int32, (1, LANES), 1) // FOX_HEAD_DIM
        for pr in range(FOX_HEADS // 2):
            inv = jnp.where(half == 0, 1.0 / l_sc[2 * pr], 1.0 / l_sc[2 * pr + 1])
            o_ref[:, pr * LANES:(pr + 1) * LANES] = acc_sc[pr] * inv


def fox_prompt(z, cum, cum_t, B, S, t):
    nq = S // t
    hd = FOX_HEADS * FOX_HEAD_DIM
    qb, kb, vb = FQ_OFF // hd, FK_OFF // hd, FV_OFF // hd
    return pl.pallas_call(
        functools.partial(_fox_prompt_kernel, t=t),
        grid=(B, nq, nq),
        in_specs=[pl.BlockSpec((t, hd), lambda b, i, j: (b * nq + i, qb)),
                  pl.BlockSpec((t, hd), lambda b, i, j: (b * nq + jnp.minimum(i, j), kb)),
                  pl.BlockSpec((t, hd), lambda b, i, j: (b * nq + jnp.minimum(i, j), vb)),
                  pl.BlockSpec((t, LANES), lambda b, i, j: (b * nq + i, 0)),
                  pl.BlockSpec((1, 8, t), lambda b, i, j: (b, 0, jnp.minimum(i, j)))],
        out_specs=pl.BlockSpec((t, hd), lambda b, i, j: (b * nq + i, 0)),
        out_shape=jax.ShapeDtypeStruct((B * S, hd), jnp.float32),
        scratch_shapes=[pltpu.VMEM((FOX_HEADS, t, LANES), jnp.float32),
                        pltpu.VMEM((FOX_HEADS, t, LANES), jnp.float32),
                        pltpu.VMEM((FOX_HEADS // 2, t, LANES), jnp.float32)],
        compiler_params=_cparams(("arbitrary", "arbitrary", "arbitrary")),
        name="fox_prompt",
    )(z, z, z, cum, cum_t)


def _rms(x, g):
    return x * lax.rsqrt(jnp.mean(x * x, axis=-1, keepdims=True) + NORM_EPS) * g


def _rope_tiled(v, cos_t, sin_t):
    return v * cos_t + pltpu.roll(v, LANES // 2, axis=1) * sin_t


def _mla_prep_kernel(cq_ref, ckv_ref, ckr_ref, cos_ref, sin_ref, gq_ref, wuq_ref, wuk_ref, gkv_ref,
                     qc_ref, kc_ref, ckvn_ref, krt_ref, qrt_ref):
    cos_t = cos_ref[...]
    sin_t = sin_ref[...]
    cqn = _rms(cq_ref[...], gq_ref[...])
    qf = _dot(_bf(cqn), wuq_ref[...])
    nq = MLA_HEADS * MLA_NOPE
    q_lat = _dot(_bf(qf[:, :nq]), wuk_ref[...])
    q_rope = _rope_tiled(qf[:, nq:], cos_t, sin_t)
    qrt_ref[...] = q_rope
    lane = lax.broadcasted_iota(jnp.int32, (1, LANES), 1)
    owner = (lane % (LANES // 2)) // (MLA_ROPE // 2)
    for h in range(MLA_HEADS):
        qc_ref[h, :, :MLA_KV_LORA] = _bf(q_lat[:, h * MLA_KV_LORA:(h + 1) * MLA_KV_LORA])
        qc_ref[h, :, MLA_KV_LORA:] = _bf(jnp.where(owner == h, q_rope, 0.0))
    ckvn = _rms(ckv_ref[...], gkv_ref[...])
    ckvn_ref[...] = ckvn
    k_rope = _rope_tiled(ckr_ref[...], cos_t, sin_t)
    krt_ref[...] = k_rope
    kc_ref[:, :MLA_KV_LORA] = _bf(ckvn)
    kc_ref[:, MLA_KV_LORA:] = _bf(k_rope)


def mla_prep(z, cos_t, sin_t, table_blocks, g_q, w_uq_r, w_uk_bd, g_kv, tm):
    N = z.shape[0]
    cw = 2 * MLA_KV_LORA
    tab = pl.BlockSpec((tm, LANES), lambda i: (i % table_blocks, 0))
    full = lambda a: pl.BlockSpec(a.shape, lambda i: (0,) * a.ndim)
    gq = g_q.reshape(1, -1)
    gkv = g_kv.reshape(1, -1)
    return pl.pallas_call(
        _mla_prep_kernel,
        grid=(N // tm,),
        in_specs=[pl.BlockSpec((tm, MLA_Q_LORA), lambda i: (i, CQ_OFF // MLA_Q_LORA)),
                  pl.BlockSpec((tm, LANES), lambda i: (i, CKV_OFF // LANES)),
                  pl.BlockSpec((tm, LANES), lambda i: (i, CKR_OFF // LANES)),
                  tab, tab, full(gq), full(w_uq_r), full(w_uk_bd), full(gkv)],
        out_specs=[pl.BlockSpec((MLA_HEADS, tm, cw), lambda i: (0, i, 0)),
                   pl.BlockSpec((tm, cw), lambda i: (i, 0)),
                   pl.BlockSpec((tm, LANES), lambda i: (i, 0)),
                   pl.BlockSpec((tm, LANES), lambda i: (i, 0)),
                   pl.BlockSpec((tm, LANES), lambda i: (i, 0))],
        out_shape=[jax.ShapeDtypeStruct((MLA_HEADS, N, cw), jnp.bfloat16),
                   jax.ShapeDtypeStruct((N, cw), jnp.bfloat16),
                   jax.ShapeDtypeStruct((N, LANES), jnp.float32),
                   jax.ShapeDtypeStruct((N, LANES), jnp.float32),
                   jax.ShapeDtypeStruct((N, LANES), jnp.float32)],
        compiler_params=_cparams(("arbitrary",)),
        name="mla_prep",
    )(z, z, z, cos_t, sin_t, gq, w_uq_r, w_uk_bd, gkv)


def _mla_prompt_kernel(q_ref, k_ref, o_ref, m_sc, l_sc, acc_sc, *, t):
    qi = pl.program_id(1)
    ki = pl.program_id(2)
    rows = MLA_HEADS * t

    @pl.when(ki == 0)
    def _():
        m_sc[...] = jnp.full_like(m_sc, NEG_BIG)
        l_sc[...] = jnp.zeros_like(l_sc)
        acc_sc[...] = jnp.zeros_like(acc_sc)

    def step(diagonal):
        q = q_ref[...].reshape(rows, 2 * MLA_KV_LORA)
        k = k_ref[...]
        s = _dot_nt(q, k) * MLA_SCALE
        if diagonal:
            row = lax.broadcasted_iota(jnp.int32, (rows, t), 0) & (t - 1)
            col = lax.broadcasted_iota(jnp.int32, (rows, t), 1)
            s = jnp.where(row >= col, s, NEG_BIG)
        m_prev = m_sc[...]
        m_new = jnp.maximum(m_prev, jnp.max(s, axis=-1, keepdims=True))
        p = jnp.exp(s - jnp.tile(m_new, (1, t // LANES)))
        alpha = jnp.exp(m_prev - m_new)
        l_sc[...] = alpha * l_sc[...] + jnp.sum(p, axis=-1, keepdims=True)
        acc_sc[...] = alpha * acc_sc[...] + _dot(_bf(p), k[:, :MLA_KV_LORA])
        m_sc[...] = m_new

    @pl.when(ki < qi)
    def _():
        step(False)

    @pl.when(ki == qi)
    def _():
        step(True)
        o_ref[...] = (acc_sc[...] / l_sc[...]).reshape(MLA_HEADS, t, MLA_KV_LORA)


def mla_prompt(qc, kc, B, S, t):
    nq = S // t
    cw = 2 * MLA_KV_LORA
    rows = MLA_HEADS * t
    return pl.pallas_call(
        functools.partial(_mla_prompt_kernel, t=t),
        grid=(B, nq, nq),
        in_specs=[pl.BlockSpec((MLA_HEADS, t, cw), lambda b, i, j: (0, b * nq + i, 0)),
                  pl.BlockSpec((t, cw), lambda b, i, j: (b * nq + jnp.minimum(i, j), 0))],
        out_specs=pl.BlockSpec((MLA_HEADS, t, MLA_KV_LORA), lambda b, i, j: (0, b * nq + i, 0)),
        out_shape=jax.ShapeDtypeStruct((MLA_HEADS, B * S, MLA_KV_LORA), jnp.float32),
        scratch_shapes=[pltpu.VMEM((rows, LANES), jnp.float32),
                        pltpu.VMEM((rows, LANES), jnp.float32),
                        pltpu.VMEM((rows, MLA_KV_LORA), jnp.float32)],
        compiler_params=_cparams(("arbitrary", "arbitrary", "arbitrary")),
        name="mla_prompt",
    )(qc, kc)


def _mlstm_kernel(*refs, L, l_real, zero_init):
    if zero_init:
        q_ref, k_ref, v_ref, kt_ref, sm_ref, smt_ref, h_ref, c_ref, n_ref, m_ref, c_sc, n_sc, m_sc = refs
    else:
        (q_ref, k_ref, v_ref, kt_ref, sm_ref, smt_ref, c0_ref, n0_ref, m0_ref,
         h_ref, c_ref, n_ref, m_ref, c_sc, n_sc, m_sc) = refs
    ci = pl.program_id(1)

    @pl.when(ci == 0)
    def _():
        if zero_init:
            c_sc[...] = jnp.zeros_like(c_sc)
            n_sc[...] = jnp.zeros_like(n_sc)
            m_sc[...] = jnp.zeros_like(m_sc)
        else:
            c_sc[...] = c0_ref[0]
            n_sc[...] = n0_ref[0]
            m_sc[...] = m0_ref[0]

    kscale = MLSTM_DK ** -0.5
    row = lax.broadcasted_iota(jnp.int32, (L, L), 0)
    col = lax.broadcasted_iota(jnp.int32, (L, L), 1)
    tri = row >= col
    real_r = lax.broadcasted_iota(jnp.int32, (1, L), 1) < l_real
    real_c = lax.broadcasted_iota(jnp.int32, (L, 1), 0) < l_real
    sm = sm_ref[...]
    smt = smt_ref[0]
    last = l_real - 1
    for h in range(MLSTM_HEADS):
        sl = slice(h * MLSTM_DK, (h + 1) * MLSTM_DK)
        qh = q_ref[:, sl]
        kh = k_ref[:, sl] * kscale
        vh = _bf(v_ref[:, sl])
        kth = kt_ref[0, h] * kscale
        ig_r = smt[SM_MI + h:SM_MI + h + 1, :]
        lf_r = smt[SM_MF + h:SM_MF + h + 1, :]
        ig_c = sm[:, SM_MI + h:SM_MI + h + 1]
        lf_c = sm[:, SM_MF + h:SM_MF + h + 1]
        b_c = jnp.sum(jnp.where(tri, lf_r, 0.0), axis=1, keepdims=True)
        b_r = jnp.sum(jnp.where(col >= row, lf_c, 0.0), axis=0, keepdims=True)
        d = jnp.where(tri, b_c - b_r + ig_r, NEG_BIG)
        m_prev = m_sc[h][:, :1]
        inter = b_c + m_prev
        m_t = jnp.maximum(inter, jnp.max(d, axis=1, keepdims=True))
        w = jnp.exp(d - m_t)
        a = jnp.exp(inter - m_t)
        qb = _bf(qh)
        qk = _dot_nt(qb, _bf(kh)) * w
        c_prev = c_sc[h]
        n_prev = n_sc[h]
        num = a * _dot(qb, _bf(c_prev)) + _dot(_bf(qk), vh)
        den = a * jnp.sum(qh * n_prev, axis=1, keepdims=True) + jnp.sum(qk, axis=1, keepdims=True)
        h_ref[:, sl] = num / jnp.maximum(jnp.abs(den), jnp.exp(-m_t))
        m_new = m_t[last:last + 1, :]
        b_last = b_c[last:last + 1, :]
        a_end = jnp.exp(b_last + m_prev - m_new)
        w_r = jnp.where(real_r, jnp.exp(b_last - b_r + ig_r - m_new), 0.0)
        w_c = jnp.where(real_c, jnp.exp(b_last - b_c + ig_c - m_new), 0.0)
        c_sc[h] = a_end * c_prev + _dot(_bf(kth * w_r), vh)
        n_sc[h] = a_end * n_prev + jnp.sum(w_c * kh, axis=0, keepdims=True)
        m_sc[h] = jnp.broadcast_to(m_new, (1, LANES))

    @pl.when(ci == pl.num_programs(1) - 1)
    def _():
        c_ref[0] = c_sc[...]
        n_ref[0] = n_sc[...]
        m_ref[0] = m_sc[...]


def mlstm_scan(q, k, v, colblocks, kt, sm, smt, state, Bm, nc, L, l_real):
    H, DK, DV = MLSTM_HEADS, MLSTM_DK, MLSTM_DV
    zero_init = state is None
    rows = Bm * nc * L
    hw = H * DK
    qs, ks_, vs = (pl.BlockSpec((L, hw), functools.partial(lambda b, c, cb: (b * nc + c, cb), cb=cb))
                   for cb in colblocks)
    in_specs = [qs, ks_, vs,
                pl.BlockSpec((1, H, DK, L), lambda b, c: (b, 0, 0, c)),
                pl.BlockSpec((L, LANES), lambda b, c: (b * nc + c, 0)),
                pl.BlockSpec((1, 16, L), lambda b, c: (b * nc + c, 0, 0))]
    args = [q, k, v, kt, sm, smt]
    st_specs = [pl.BlockSpec((1, H, DK, DV), lambda b, c: (b, 0, 0, 0)),
                pl.BlockSpec((1, H, 1, DK), lambda b, c: (b, 0, 0, 0)),
                pl.BlockSpec((1, H, 1, LANES), lambda b, c: (b, 0, 0, 0))]
    if not zero_init:
        in_specs += st_specs
        args += list(state)
    return pl.pallas_call(
        functools.partial(_mlstm_kernel, L=L, l_real=l_real, zero_init=zero_init),
        grid=(Bm, nc),
        in_specs=in_specs,
        out_specs=[pl.BlockSpec((L, hw), lambda b, c: (b * nc + c, 0))] + st_specs,
        out_shape=[jax.ShapeDtypeStruct((rows, hw), jnp.float32),
                   jax.ShapeDtypeStruct((Bm, H, DK, DV), jnp.float32),
                   jax.ShapeDtypeStruct((Bm, H, 1, DK), jnp.float32),
                   jax.ShapeDtypeStruct((Bm, H, 1, LANES), jnp.float32)],
        scratch_shapes=[pltpu.VMEM((H, DK, DV), jnp.float32),
                        pltpu.VMEM((H, 1, DK), jnp.float32),
                        pltpu.VMEM((H, 1, LANES), jnp.float32)],
        compiler_params=_cparams(("arbitrary", "arbitrary")),
        name="mlstm_scan",
    )(*args)


def _merge_kernel(x_ref, gate_ref, sh2_ref, sc2_ref, ofox_ref, hml_ref, mo_ref, zg0_ref, zg1_ref, zg2_ref,
                  olat_ref, wbf_ref, wbm_ref, wuv_ref, wbc_ref, wout_ref, g2_ref, x1_ref, xn2_ref):
    sig = lambda u: 1.0 / (1.0 + jnp.exp(-u))
    o_ml = sig(mo_ref[...]) * hml_ref[...]
    o_lat = jnp.concatenate([olat_ref[h] for h in range(MLA_HEADS)], axis=1)
    o_mla = _dot(_bf(o_lat), wuv_ref[...])
    merged = (sig(zg0_ref[...]) * _dot(_bf(ofox_ref[...]), wbf_ref[...])
              + sig(zg1_ref[...]) * _dot(_bf(o_ml), wbm_ref[...])
              + sig(zg2_ref[...]) * _dot(_bf(o_mla), wbc_ref[...]))
    x1 = x_ref[...] + gate_ref[0] * _dot(_bf(merged), wout_ref[...])
    x1_ref[...] = x1
    xn2_ref[...] = _modulate(x1, g2_ref[...], sh2_ref[0], sc2_ref[0])


def merge_block(x, gate, shift2, scale2, o_fox, h_ml, z, o_lat, wbf, wbm, wuv_bd, wbc, wout, g2,
                rows_per_group, tm):
    N, D = x.shape
    G, R, _ = gate.shape
    if R == 1:
        bpg = rows_per_group // tm
        mod_spec = pl.BlockSpec((1, 1, D), lambda i: (i // bpg, 0, 0))
    else:
        mod_spec = pl.BlockSpec((1, tm, D), lambda i: (0, i, 0))
    full = lambda a: pl.BlockSpec(a.shape, lambda i: (0,) * a.ndim)
    rowblk = lambda w, cb=0: pl.BlockSpec((tm, w), lambda i: (i, cb))
    g2r = g2.reshape(1, D)
    return pl.pallas_call(
        _merge_kernel,
        grid=(N // tm,),
        in_specs=[rowblk(D), mod_spec, mod_spec, mod_spec,
                  rowblk(256), rowblk(512), rowblk(512, MO_OFF // 512),
                  rowblk(D, 0), rowblk(D, 1), rowblk(D, 2),
                  pl.BlockSpec((MLA_HEADS, tm, MLA_KV_LORA), lambda i: (0, i, 0)),
                  full(wbf), full(wbm), full(wuv_bd), full(wbc), full(wout), full(g2r)],
        out_specs=[rowblk(D), rowblk(D)],
        out_shape=[jax.ShapeDtypeStruct((N, D), jnp.float32),
                   jax.ShapeDtypeStruct((N, D), jnp.float32)],
        compiler_params=_cparams(("arbitrary",)),
        name="merge_block",
    )(x, gate, shift2, scale2, o_fox, h_ml, z, z, z, z, o_lat, wbf, wbm, wuv_bd, wbc, wout, g2r)


def _pool_logf_kernel(x_ref, suf_ref, tot_ref):
    tm, H, n = x_ref.shape
    r = lax.broadcasted_iota(jnp.int32, (n, n), 0)
    c = lax.broadcasted_iota(jnp.int32, (n, n), 1)
    later = _bf(jnp.where(r > c, 1.0, 0.0))
    ones = jnp.ones((n, n), jnp.bfloat16)
    h, m, l = _split3(x_ref[...].reshape(tm * H, n))
    suf_ref[...] = (_dot(h, later) + _dot(m, later) + _dot(l, later)).reshape(tm, H, n)
    tot_ref[...] = (_dot(h, ones) + _dot(m, ones) + _dot(l, ones)).reshape(tm, H, n)


def pool_logf_suffix(logf_t, tm):
    n_rows, H, n = logf_t.shape
    assert n_rows % tm == 0
    spec = pl.BlockSpec((tm, H, n), lambda i: (i, 0, 0))
    shape = jax.ShapeDtypeStruct((n_rows, H, n), jnp.float32)
    return pl.pallas_call(
        _pool_logf_kernel,
        grid=(n_rows // tm,),
        in_specs=[spec],
        out_specs=[spec, spec],
        out_shape=[shape, shape],
        compiler_params=_cparams(("arbitrary",)),
        name="pool_logf_suffix",
    )(logf_t)


def _online_update_many(scores, weigh, m_sc, l_sc, acc_sc):
    m_prev = m_sc[...]
    m_new = m_prev
    for s in scores:
        m_new = jnp.maximum(m_new, jnp.max(s, axis=-1, keepdims=True))
    alpha = jnp.exp(m_prev - m_new)
    l = alpha * l_sc[...]
    acc = alpha * acc_sc[...]
    for s, fn in zip(scores, weigh):
        p = jnp.exp(s - m_new)
        l = l + jnp.sum(p, axis=-1, keepdims=True)
        acc = acc + fn(p)
    l_sc[...] = l
    acc_sc[...] = acc
    m_sc[...] = m_new


def _fox_decode_kernel(pt_ref, q_ref, kn_ref, vn_ref, lfr_ref, lfc_ref, *rest, P, T):
    k_refs, v_refs, suf_refs, tot_refs = (rest[j * P:(j + 1) * P] for j in range(4))
    o_ref, m_sc, l_sc, acc_sc, carry_sc = rest[4 * P:]
    s_id = pl.program_id(1)
    R = FOX_HEADS * T
    scale = FOX_HEAD_DIM ** -0.5
    q = q_ref[0]
    qb = _bf(q)

    @pl.when(s_id == 0)
    def _():
        r = lax.broadcasted_iota(jnp.int32, (R, R), 0)
        c = lax.broadcasted_iota(jnp.int32, (R, R), 1)
        pre = ((r % FOX_HEADS) == (c % FOX_HEADS)) & (r <= c)
        cn = jnp.sum(jnp.where(pre, lfc_ref[0], 0.0), axis=0, keepdims=True)
        s = _dot_nt(qb, _bf(kn_ref[0])) * scale - cn
        ok = ((c % FOX_HEADS) == (r // T)) & ((c // FOX_HEADS) <= (r % T))
        s = jnp.where(ok, s, NEG_BIG)
        m = jnp.max(s, axis=-1, keepdims=True)
        p = jnp.exp(s - m)
        m_sc[...] = m
        l_sc[...] = jnp.sum(p, axis=-1, keepdims=True)
        acc_sc[...] = _dot(_bf(p), _bf(vn_ref[0]))
        carry_sc[...] = jnp.zeros_like(carry_sc)

    rowhead = lax.broadcasted_iota(jnp.int32, (R, 1), 0) // T
    mine = [rowhead == h for h in range(FOX_HEADS)]
    q_h = [_bf(jnp.where(mine[h], q, 0.0)) for h in range(FOX_HEADS)]

    def per_row(a):
        out = jnp.zeros((R, a.shape[1]), jnp.float32)
        for h in range(FOX_HEADS):
            out = jnp.where(mine[h], a[h:h + 1, :], out)
        return out

    def weigh_page(i):
        def fn(p):
            out = jnp.zeros((R, FOX_HEAD_DIM), jnp.float32)
            for h in range(FOX_HEADS):
                out = out + _dot_nt(_bf(jnp.where(mine[h], p, 0.0)), _bf(v_refs[i][0, h]))
            return out
        return fn

    carry = carry_sc[...]
    scores = []
    for i in range(P):
        s = jnp.zeros((R, carry.shape[1]), jnp.float32)
        for h in range(FOX_HEADS):
            s = s + _dot(q_h[h], _bf(k_refs[i][0, h]))
        scores.append(s * scale + (per_row(suf_refs[i][0]) + carry))
        carry = carry + per_row(tot_refs[i][0])
    carry_sc[...] = carry
    _online_update_many(scores, [weigh_page(i) for i in range(P)], m_sc, l_sc, acc_sc)

    @pl.when(s_id == pl.num_programs(1) - 1)
    def _():
        o_ref[0] = acc_sc[...] / l_sc[...]


def fox_decode(page_table, q, k_new, v_new, lf_row, lf_col, pool_kt, pool_vt, pool_suf, pool_tot,
               page_off, P):
    DB, R, dh = q.shape
    n_pages = page_table.shape[1]
    T = R // FOX_HEADS
    assert n_pages % P == 0
    steps = n_pages // P
    page = pool_kt.shape[3]

    def page_map(i, nd):
        return lambda b, s, pt: (page_off + pt[b, n_pages - 1 - (s * P + i)],) + (0,) * nd

    per_b = lambda shape: pl.BlockSpec((1,) + shape, lambda b, s, pt: (b, 0, 0))
    in_specs = [per_b((R, dh)), per_b((R, dh)), per_b((R, dh)), per_b((1, R)), per_b((R, 1))]
    in_specs += [pl.BlockSpec((1, FOX_HEADS, dh, page), page_map(i, 3)) for i in range(P)]
    in_specs += [pl.BlockSpec((1, FOX_HEADS, dh, page), page_map(i, 3)) for i in range(P)]
    in_specs += [pl.BlockSpec((1, FOX_HEADS, page), page_map(i, 2)) for i in range(P)]
    in_specs += [pl.BlockSpec((1, FOX_HEADS, page), page_map(i, 2)) for i in range(P)]
    grid_spec = pltpu.PrefetchScalarGridSpec(
        num_scalar_prefetch=1, grid=(DB, steps), in_specs=in_specs,
        out_specs=per_b((R, dh)),
        scratch_shapes=[pltpu.VMEM((R, 1), jnp.float32), pltpu.VMEM((R, 1), jnp.float32),
                        pltpu.VMEM((R, dh), jnp.float32), pltpu.VMEM((R, page), jnp.float32)])
    return pl.pallas_call(
        functools.partial(_fox_decode_kernel, P=P, T=T),
        grid_spec=grid_spec,
        out_shape=jax.ShapeDtypeStruct((DB, R, dh), jnp.float32),
        compiler_params=_cparams(("arbitrary", "arbitrary")),
        name="fox_decode",
    )(page_table, q, k_new, v_new, lf_row, lf_col, *([pool_kt] * P), *([pool_vt] * P),
      *([pool_suf] * P), *([pool_tot] * P))


def _mla_decode_kernel(pt_ref, ql_ref, qr_ref, cn_ref, kn_ref, *rest, P, T):
    c_refs, k_refs = rest[:P], rest[P:2 * P]
    o_ref, m_sc, l_sc, acc_sc = rest[2 * P:]
    s_id = pl.program_id(1)
    R = MLA_HEADS * T
    ql = _bf(ql_ref[0])
    qr = _bf(qr_ref[0])

    @pl.when(s_id == 0)
    def _():
        cn = _bf(cn_ref[0])
        s = (_dot_nt(ql, cn) + _dot_nt(qr, _bf(kn_ref[0]))) * MLA_SCALE
        r = lax.broadcasted_iota(jnp.int32, (R, T), 0)
        c = lax.broadcasted_iota(jnp.int32, (R, T), 1)
        s = jnp.where(c <= (r % T), s, NEG_BIG)
        m = jnp.max(s, axis=-1, keepdims=True)
        p = jnp.exp(s - m)
        m_sc[...] = m
        l_sc[...] = jnp.sum(p, axis=-1, keepdims=True)
        acc_sc[...] = _dot(_bf(p), cn)

    pages = [_bf(c_refs[i][0]) for i in range(P)]
    scores = [(_dot_nt(ql, pages[i]) + _dot(qr, _bf(k_refs[i][0]))) * MLA_SCALE for i in range(P)]
    _online_update_many(scores, [functools.partial(lambda p, c: _dot(_bf(p), c), c=pages[i]) for i in range(P)],
                        m_sc, l_sc, acc_sc)

    @pl.when(s_id == pl.num_programs(1) - 1)
    def _():
        o_ref[0] = acc_sc[...] / l_sc[...]


def mla_decode(page_table, q_lat, q_rope, ckv_new, kr_new, pool_ckv, pool_krt, page_off, P):
    DB, R, C = q_lat.shape
    n_pages = page_table.shape[1]
    T = R // MLA_HEADS
    assert n_pages % P == 0
    steps = n_pages // P
    ps, rr = pool_ckv.shape[1], pool_krt.shape[1]

    def page_map(i):
        return lambda b, s, pt: (page_off + pt[b, s * P + i], 0, 0)

    per_b = lambda shape: pl.BlockSpec((1,) + shape, lambda b, s, pt: (b, 0, 0))
    in_specs = [per_b((R, C)), per_b((R, rr)), per_b((T, C)), per_b((T, rr))]
    in_specs += [pl.BlockSpec((1, ps, C), page_map(i)) for i in range(P)]
    in_specs += [pl.BlockSpec((1, rr, ps), page_map(i)) for i in range(P)]
    grid_spec = pltpu.PrefetchScalarGridSpec(
        num_scalar_prefetch=1, grid=(DB, steps), in_specs=in_specs,
        out_specs=per_b((R, C)),
        scratch_shapes=[pltpu.VMEM((R, 1), jnp.float32), pltpu.VMEM((R, 1), jnp.float32),
                        pltpu.VMEM((R, C), jnp.float32)])
    return pl.pallas_call(
        functools.partial(_mla_decode_kernel, P=P, T=T),
        grid_spec=grid_spec,
        out_shape=jax.ShapeDtypeStruct((DB, R, C), jnp.float32),
        compiler_params=_cparams(("arbitrary", "arbitrary")),
        name="mla_decode",
    )(page_table, q_lat, q_rope, ckv_new, kr_new, *([pool_ckv] * P), *([pool_krt] * P))


def _top_rows(vals, k):
    n = vals.shape[0]
    pos = lax.broadcasted_iota(jnp.int32, vals.shape, 0).astype(jnp.float32)
    top_v, top_p = [], []
    for _ in range(k):
        m = jnp.max(vals, axis=0, keepdims=True)
        at = jnp.min(jnp.where(vals == m, pos, float(n)), axis=0, keepdims=True)
        top_v.append(m)
        top_p.append(at)
        vals = jnp.where(pos == at, -jnp.inf, vals)
    return jnp.concatenate(top_v, axis=0), jnp.concatenate(top_p, axis=0)


def _pick_rows(table, at):
    out = jnp.zeros(at.shape, table.dtype)
    for a in range(table.shape[0]):
        out = jnp.where(at == float(a), table[a:a + 1, :], out)
    return out


def _peer_select_kernel(x_ref, wh_ref, wl_ref, skh_ref, skl_ref, idx_ref, g_ref):
    x = x_ref[...]
    xh = _bf(x)
    xl = _bf(x - xh.astype(jnp.float32))
    wh = wh_ref[...]
    qt = _dot_nt(wh, xh) + _dot_nt(wh, xl) + _dot_nt(wl_ref[...], xh)
    half = PEER_KEY_DIM // 2
    K = PEER_TOPK
    for h in range(PEER_HEADS):
        tops = []
        for p in range(2):
            j = 2 * h + p
            q = qt[j * half:(j + 1) * half, :]
            qh = _bf(q)
            ql = _bf(q - qh.astype(jnp.float32))
            skh = skh_ref[j]
            s = _dot(skh, qh) + _dot(skh, ql) + _dot(skl_ref[j], qh)
            tops.append(_top_rows(s, K))
        (s1, i1), (s2, i2) = tops
        cand_s = jnp.concatenate([s1[a:a + 1, :] + s2 for a in range(K)], axis=0)
        best_s, best_j = _top_rows(cand_s, K)
        ja = jnp.floor(best_j * (1.0 / K))
        jb = best_j - ja * K
        best_i = _pick_rows(i1, ja) * PEER_N_KEYS + _pick_rows(i2, jb)
        e = jnp.exp(best_s - best_s[0:1, :])
        g_ref[0, h * K:(h + 1) * K, :] = e / jnp.sum(e, axis=0, keepdims=True)
        idx_ref[0, h * K:(h + 1) * K, :] = best_i.astype(jnp.int32)


def peer_select(xn, wq_t_hi, wq_t_lo, sk_hi, sk_lo, tm):
    N, D = xn.shape
    HK = PEER_HEADS * PEER_TOPK
    full = lambda a: pl.BlockSpec(a.shape, lambda i: (0,) * a.ndim)
    return pl.pallas_call(
        _peer_select_kernel,
        grid=(N // tm,),
        in_specs=[pl.BlockSpec((tm, D), lambda i: (i, 0)),
                  full(wq_t_hi), full(wq_t_lo), full(sk_hi), full(sk_lo)],
        out_specs=[pl.BlockSpec((1, HK, tm), lambda i: (i, 0, 0)),
                   pl.BlockSpec((1, HK, tm), lambda i: (i, 0, 0))],
        out_shape=[jax.ShapeDtypeStruct((N // tm, HK, tm), jnp.int32),
                   jax.ShapeDtypeStruct((N // tm, HK, tm), jnp.float32)],
        compiler_params=_cparams(("arbitrary",)),
        name="peer_select",
    )(xn, wq_t_hi, wq_t_lo, sk_hi, sk_lo)


PEER_ROWS = PEER_HEADS * PEER_TOPK
PK_SUB = 4


def _unpack_row(w):
    hi = lax.bitcast_convert_type(w & jnp.uint32(0xFFFF0000), jnp.float32)
    lo = lax.bitcast_convert_type(w << 16, jnp.float32)
    return hi, lo


_FOLD_SLOT_ROW = (3, 7, 1, 5, 2, 6, 0, 4)


def _fold8(tiles):
    sub = lax.broadcasted_iota(jnp.int32, (8, LANES), 0)
    pairs = [jnp.concatenate([tiles[2 * i], tiles[2 * i + 1]], axis=0) for i in range(4)]
    t = [c + pltpu.roll(c, 2, 0) for c in pairs]
    keep2 = (sub & 2) != 0
    u = [jnp.where(keep2, t[0], pltpu.roll(t[1], 6, 0)), jnp.where(keep2, t[2], pltpu.roll(t[3], 6, 0))]
    v = [w + pltpu.roll(w, 1, 0) for w in u]
    return jnp.where((sub & 1) != 0, v[0], pltpu.roll(v[1], 7, 0))


def _peer_act_kernel(idx_ref, x_ref, tab_ref, o_ref, *, tb):
    lane = lax.broadcasted_iota(jnp.int32, (8, LANES), 1)

    def token(t, carry):
        xh = x_ref[t, 0:PK_SUB, :]
        xl = x_ref[t, PK_SUB:2 * PK_SUB, :]
        tile = jnp.zeros((8, LANES), jnp.float32)
        for gi in range(PEER_ROWS // 8):
            prods = []
            for s in _FOLD_SLOT_ROW:
                hi, lo = _unpack_row(tab_ref[idx_ref[t, gi * 8 + s]])
                prods.append(hi * xh + lo * xl)
            grp = _fold8(prods)
            tile = jnp.where(lane == gi, jnp.sum(grp, axis=1, keepdims=True), tile)
        o_ref[t] = tile
        return carry

    lax.fori_loop(0, tb, token, 0)


def peer_act(idx, x8, u_pk, tb):
    N = idx.shape[0]
    return pl.pallas_call(
        functools.partial(_peer_act_kernel, tb=tb),
        grid=(N // tb,),
        in_specs=[pl.BlockSpec((tb, PEER_ROWS), lambda i: (i, 0), memory_space=pltpu.SMEM),
                  pl.BlockSpec((tb, 8, LANES), lambda i: (i, 0, 0)),
                  pl.BlockSpec(u_pk.shape, lambda i: (0, 0, 0), pipeline_mode=pl.Buffered(1))],
        out_specs=pl.BlockSpec((tb, 8, LANES), lambda i: (i, 0, 0)),
        out_shape=jax.ShapeDtypeStruct((N, 8, LANES), jnp.float32),
        compiler_params=pltpu.CompilerParams(dimension_semantics=("arbitrary",),
                                             vmem_limit_bytes=_PEER_VMEM_LIMIT),
        name="peer_act",
    )(idx, x8, u_pk)


def _peer_mix_kernel(idx_ref, coef_ref, tab_ref, o_ref, *, tb):
    def token(t, carry):
        acc_h = jnp.zeros((PK_SUB, LANES), jnp.float32)
        acc_l = jnp.zeros((PK_SUB, LANES), jnp.float32)
        for r in range(PEER_ROWS):
            hi, lo = _unpack_row(tab_ref[idx_ref[t, r]])
            c = coef_ref[t, r]
            acc_h = acc_h + c * hi
            acc_l = acc_l + c * lo
        o_ref[t] = jnp.concatenate([acc_h, acc_l], axis=0)
        return carry

    lax.fori_loop(0, tb, token, 0)


def peer_mix(idx, coef, v_pk, tb):
    N = idx.shape[0]
    return pl.pallas_call(
        functools.partial(_peer_mix_kernel, tb=tb),
        grid=(N // tb,),
        in_specs=[pl.BlockSpec((tb, PEER_ROWS), lambda i: (i, 0), memory_space=pltpu.SMEM),
                  pl.BlockSpec((tb, PEER_ROWS), lambda i: (i, 0), memory_space=pltpu.SMEM),
                  pl.BlockSpec(v_pk.shape, lambda i: (0, 0, 0), pipeline_mode=pl.Buffered(1))],
        out_specs=pl.BlockSpec((tb, 8, LANES), lambda i: (i, 0, 0)),
        out_shape=jax.ShapeDtypeStruct((N, 8, LANES), jnp.float32),
        compiler_params=pltpu.CompilerParams(dimension_semantics=("arbitrary",),
                                             vmem_limit_bytes=_PEER_VMEM_LIMIT),
        name="peer_mix",
    )(idx, coef, v_pk)


def _peer_coef_kernel(a_ref, g_ref, c_ref):
    a = a_ref[...]
    gelu = 0.5 * a * (1.0 + jnp.tanh(math.sqrt(2.0 / math.pi) * (a + 0.044715 * a * a * a)))
    c_ref[...] = g_ref[...] * gelu


def peer_coef(act, g, tm):
    N, Rr = act.shape
    spec = pl.BlockSpec((tm, Rr), lambda i: (i, 0))
    return pl.pallas_call(
        _peer_coef_kernel, grid=(N // tm,), in_specs=[spec, spec], out_specs=spec,
        out_shape=jax.ShapeDtypeStruct((N, Rr), jnp.float32),
        compiler_params=_cparams(("arbitrary",)), name="peer_coef",
    )(act, g)


def _residual_kernel(x_ref, gate_ref, y_ref, g_ref, o_ref, *, final_norm):
    x = x_ref[...] + gate_ref[0] * y_ref[...]
    if final_norm:
        x = _rms(x, g_ref[...])
    o_ref[...] = x


def residual_out(x, gate, y, g_final, rows_per_group, tm, final_norm):
    N, D = x.shape
    G, R, _ = gate.shape
    if R == 1:
        bpg = rows_per_group // tm
        mod_spec = pl.BlockSpec((1, 1, D), lambda i: (i // bpg, 0, 0))
    else:
        mod_spec = pl.BlockSpec((1, tm, D), lambda i: (0, i, 0))
    spec = pl.BlockSpec((tm, D), lambda i: (i, 0))
    return pl.pallas_call(
        functools.partial(_residual_kernel, final_norm=final_norm),
        grid=(N // tm,),
        in_specs=[spec, mod_spec, spec, pl.BlockSpec((1, D), lambda i: (0, 0))],
        out_specs=spec,
        out_shape=jax.ShapeDtypeStruct((N, D), jnp.float32),
        compiler_params=_cparams(("arbitrary",)), name="residual_out",
    )(x, gate, y, g_final.reshape(1, D))


def pack_expert_rows(w):
    E, D = w.shape
    b = lax.bitcast_convert_type(w.astype(jnp.bfloat16), jnp.uint16).astype(jnp.uint32)
    pk = (b[:, :D // 2] << 16) | b[:, D // 2:]
    return pk.reshape(E, PK_SUB, LANES)


def rope_tables(pos):
    half = MLA_ROPE // 2
    freqs = ROPE_THETA ** (-jnp.arange(half, dtype=jnp.float32) / half)
    ang = pos.astype(jnp.float32)[:, None] * freqs[None, :]
    cos, sin = jnp.cos(ang), jnp.sin(ang)
    return jnp.tile(cos, (1, 8)), jnp.concatenate([-sin] * 4 + [sin] * 4, axis=1)


def mla_weights(w_uq, w_uk, w_uv):
    H, NP, hr = MLA_HEADS, MLA_NOPE, MLA_ROPE // 2
    nope = w_uq[:, :, :NP].reshape(MLA_Q_LORA, H * NP)
    x1 = w_uq[:, :, NP:NP + hr].reshape(MLA_Q_LORA, H * hr)
    x2 = w_uq[:, :, NP + hr:].reshape(MLA_Q_LORA, H * hr)
    w_uq_r = jnp.concatenate([nope, x1, x2], axis=1).astype(jnp.bfloat16)
    w_uk_bd = jnp.zeros((H * NP, H * MLA_KV_LORA), jnp.float32)
    w_uv_bd = jnp.zeros((H * MLA_KV_LORA, H * MLA_V), jnp.float32)
    for h in range(H):
        w_uk_bd = w_uk_bd.at[h * NP:(h + 1) * NP, h * MLA_KV_LORA:(h + 1) * MLA_KV_LORA].set(w_uk[:, h, :].T)
        w_uv_bd = w_uv_bd.at[h * MLA_KV_LORA:(h + 1) * MLA_KV_LORA, h * MLA_V:(h + 1) * MLA_V].set(w_uv[:, h, :])
    return w_uq_r, w_uk_bd.astype(jnp.bfloat16), w_uv_bd.astype(jnp.bfloat16)


_IN_SIZES = (256, 256, 256, 4, 512, 512, 512, 512, 4, 4, 256, 128, 32, 3072)


def pad_w_in(w_in):
    D = w_in.shape[0]
    offs = [0]
    for s in _IN_SIZES:
        offs.append(offs[-1] + s)
    fq, fk, fv, ff, mq, mk, mv, mo, mi, mf, cq, ckv, ckr, zg = (
        w_in[:, offs[i]:offs[i + 1]] for i in range(len(_IN_SIZES)))
    hr = MLA_ROPE // 2
    ckr_t = jnp.concatenate([ckr[:, :hr]] * 4 + [ckr[:, hr:]] * 4, axis=1)
    small = jnp.concatenate([ff, mi, mf, jnp.zeros((D, LANES - 12), w_in.dtype)], axis=1)
    pad = jnp.zeros((D, LANES), w_in.dtype)
    w = jnp.concatenate([zg, mq, mk, mv, mo, fq, fk, fv, cq, ckv, ckr_t, small, pad], axis=1)
    return w.astype(jnp.bfloat16)


def small_bias_row(b_fox_f, b_mlstm_i, b_mlstm_f):
    return jnp.concatenate([b_fox_f, b_mlstm_i, b_mlstm_f,
                            jnp.zeros((LANES - 12,), jnp.float32)]).reshape(1, LANES)


def _split2(w):
    h = w.astype(jnp.bfloat16)
    return h, (w - h.astype(jnp.float32)).astype(jnp.bfloat16)


def _krope_rows(krt):
    hr = MLA_ROPE // 2
    return jnp.concatenate([krt[:, :hr], krt[:, LANES // 2:LANES // 2 + hr]], axis=1)


def _token_major(a):
    nb, r, tm = a.shape
    return a.transpose(0, 2, 1).reshape(nb * tm, r)


PROMPT_TM = 512
ATTN_T_FOX = 512
ATTN_T_MLA = 512
SAMPLE_TM = 256
PEER_TB = 128
FOX_DECODE_PAGES = 16
MLA_DECODE_PAGES = 32
POOL_LOGF_TM = 128
SAMPLE_CHUNK = 16


def _peer_ffn(xn2, lw, tm_sel):
    N, D = xn2.shape
    idx_t, g_t = peer_select(xn2, lw["wq_hi"], lw["wq_lo"], lw["sk_hi"], lw["sk_lo"], tm_sel)
    idx, g = _token_major(idx_t), _token_major(g_t)
    a8 = peer_act(idx, xn2.reshape(N, 8, LANES), lw["u_pk"], PEER_TB)
    act = a8[:, :, :PEER_ROWS // 8].transpose(0, 2, 1).reshape(N, PEER_ROWS)
    coef = peer_coef(act, g, tm_sel)
    return peer_mix(idx, coef, lw["v_pk"], PEER_TB).reshape(N, D)


def kernel(x_prompt, x_sample, c_prompt, c_sample, cache_fox_k, cache_fox_v, cache_fox_logf, cache_mla_ckv, cache_mla_krope, state_mlstm_C, state_mlstm_n, state_mlstm_m, page_table, w_ada, b_ada, g_norm_mix, g_norm_ffn, w_in, b_fox_f, b_mlstm_i, b_mlstm_f, g_mla_q, g_mla_kv, w_mla_uq, w_mla_uk, w_mla_uv, w_branch_fox, w_branch_mlstm, w_branch_mla, w_out, w_peer_q, peer_subkeys, peer_u, peer_v, g_final):
    B, S, D = x_prompt.shape
    DB, T, _ = x_sample.shape
    depth = w_in.shape[0]
    n_pool, page = cache_fox_k.shape[1], cache_fox_k.shape[2]
    n_pages = page_table.shape[1]
    past = n_pages * page
    Np, Ns = B * S, DB * T
    H = FOX_HEADS
    LP = SAMPLE_CHUNK

    xp = x_prompt.reshape(Np, D)
    xs = x_sample.reshape(Ns, D)
    n_c = B + DB
    c_all = jnp.concatenate([c_prompt, c_sample, jnp.zeros((-n_c % 8, D), jnp.float32)], axis=0)

    pool_kt = cache_fox_k.transpose(0, 1, 3, 4, 2).reshape(depth * n_pool, H, FOX_HEAD_DIM, page)
    pool_vt = cache_fox_v.transpose(0, 1, 3, 4, 2).reshape(depth * n_pool, H, FOX_HEAD_DIM, page)
    pool_suf, pool_tot = pool_logf_suffix(
        cache_fox_logf.transpose(0, 1, 3, 2).reshape(depth * n_pool, H, page), POOL_LOGF_TM)
    pool_ckv = cache_mla_ckv.reshape(depth * n_pool, page, MLA_KV_LORA)
    pool_krt = cache_mla_krope.transpose(0, 1, 3, 2).reshape(depth * n_pool, MLA_ROPE, page)
    cos_p, sin_p = rope_tables(jnp.arange(S))
    cos_s, sin_s = (jnp.tile(a, (DB, 1)) for a in rope_tables(past + jnp.arange(T)))

    prompt_rows, sample_rows = [], []
    for l in range(depth):
        w_uq_r, w_uk_bd, w_uv_bd = mla_weights(w_mla_uq[l], w_mla_uk[l], w_mla_uv[l])
        wq_hi, wq_lo = _split2(w_peer_q[l].T)
        sk_hi, sk_lo = _split2(peer_subkeys[l].reshape(2 * PEER_HEADS, PEER_N_KEYS, PEER_KEY_DIM // 2))
        lw = dict(wq_hi=wq_hi, wq_lo=wq_lo, sk_hi=sk_hi, sk_lo=sk_lo,
                  u_pk=pack_expert_rows(peer_u[l]), v_pk=pack_expert_rows(peer_v[l]))
        w_bf = pad_w_in(w_in[l])
        bias_row = small_bias_row(b_fox_f[l], b_mlstm_i[l], b_mlstm_f[l])
        wbf, wbm, wbc, wout = (_bf(w_branch_fox[l]), _bf(w_branch_mlstm[l]),
                               _bf(w_branch_mla[l]), _bf(w_out[l]))
        last = l == depth - 1

        mods = ada_ln(c_all, w_ada[l], b_ada[l])
        mp = mods[:B].reshape(B, 6, D)
        ms = mods[B:n_c].reshape(DB, 6, D)
        mod_p = [mp[:, i][:, None, :] for i in range(6)]
        mod_s = [jnp.repeat(ms[:, i], T, axis=0)[None] for i in range(6)]

        z = in_proj(xp, mod_p[0], mod_p[1], g_norm_mix[l], w_bf, S, PROMPT_TM)
        sm, cum = small_gates(z, bias_row, B, S, 256)
        cum_t = jnp.pad(jnp.swapaxes(cum[:, :H].reshape(B, S, H), 1, 2), ((0, 0), (0, 8 - H), (0, 0)))
        o_fox = fox_prompt(z, cum, cum_t, B, S, ATTN_T_FOX)
        qc, kc, ckvn, krt, _ = mla_prep(z, cos_p, sin_p, S // PROMPT_TM, g_mla_q[l], w_uq_r, w_uk_bd,
                                        g_mla_kv[l], PROMPT_TM)
        o_lat = mla_prompt(qc, kc, B, S, ATTN_T_MLA)
        nc = S // MLSTM_CHUNK
        hw = MLSTM_HEADS * MLSTM_DK
        kt = z[:, MK_OFF:MK_OFF + hw].reshape(B, S, MLSTM_HEADS, MLSTM_DK).transpose(0, 2, 3, 1)
        smt = sm[:, :16].reshape(B * nc, MLSTM_CHUNK, 16).transpose(0, 2, 1)
        h_ml, C_p, n_p, m_p = mlstm_scan(z, z, z, (MQ_OFF // hw, MK_OFF // hw, MV_OFF // hw), kt, sm, smt,
                                         None, B, nc, MLSTM_CHUNK, MLSTM_CHUNK)
        x1, xn2 = merge_block(xp, mod_p[2], mod_p[3], mod_p[4], o_fox, h_ml, z, o_lat,
                              wbf, wbm, w_uv_bd, wbc, wout, g_norm_ffn[l], S, 256)
        y = _peer_ffn(xn2, lw, 256)
        xp = residual_out(x1, mod_p[5], y, g_final, S, PROMPT_TM, last)
        prompt_rows.append((
            z[:, FK_OFF:FK_OFF + 256].reshape(B, S, H, FOX_HEAD_DIM),
            z[:, FV_OFF:FV_OFF + 256].reshape(B, S, H, FOX_HEAD_DIM),
            sm[:, :H].reshape(B, S, H),
            ckvn.reshape(B, S, MLA_KV_LORA),
            _krope_rows(krt).reshape(B, S, MLA_ROPE),
            C_p, n_p[:, :, 0, :], m_p[:, :, 0, 0]))

        zs = in_proj(xs, mod_s[0], mod_s[1], g_norm_mix[l], w_bf, Ns, SAMPLE_TM)
        sm_s, _ = small_gates(zs, bias_row, 1, Ns, SAMPLE_TM)
        fq = zs[:, FQ_OFF:FQ_OFF + 256].reshape(DB, T, H, FOX_HEAD_DIM).transpose(0, 2, 1, 3)
        fk_s = zs[:, FK_OFF:FK_OFF + 256]
        fv_s = zs[:, FV_OFF:FV_OFF + 256]
        lf_s = sm_s[:, :H]
        o_f = fox_decode(page_table, fq.reshape(DB, H * T, FOX_HEAD_DIM),
                         fk_s.reshape(DB, T * H, FOX_HEAD_DIM), fv_s.reshape(DB, T * H, FOX_HEAD_DIM),
                         lf_s.reshape(DB, 1, T * H), lf_s.reshape(DB, T * H, 1),
                         pool_kt, pool_vt, pool_suf, pool_tot, l * n_pool, FOX_DECODE_PAGES)
        o_fox_s = o_f.reshape(DB, H, T, FOX_HEAD_DIM).transpose(0, 2, 1, 3).reshape(Ns, H * FOX_HEAD_DIM)
        qc_s, _, ckvn_s, krt_s, qrt_s = mla_prep(zs, cos_s, sin_s, Ns // SAMPLE_TM, g_mla_q[l], w_uq_r,
                                                 w_uk_bd, g_mla_kv[l], SAMPLE_TM)
        q_lat_s = qc_s[:, :, :MLA_KV_LORA].reshape(MLA_HEADS, DB, T, MLA_KV_LORA).transpose(1, 0, 2, 3)
        hr = MLA_ROPE // 2
        q_rope_s = qrt_s.reshape(DB, T, 2, MLA_HEADS, hr).transpose(0, 3, 1, 2, 4)
        kr_s = _krope_rows(krt_s)
        o_l = mla_decode(page_table, q_lat_s.reshape(DB, MLA_HEADS * T, MLA_KV_LORA),
                         q_rope_s.reshape(DB, MLA_HEADS * T, MLA_ROPE),
                         ckvn_s.reshape(DB, T, MLA_KV_LORA), kr_s.reshape(DB, T, MLA_ROPE),
                         pool_ckv, pool_krt, l * n_pool, MLA_DECODE_PAGES)
        o_lat_s = o_l.reshape(DB, MLA_HEADS, T, MLA_KV_LORA).transpose(1, 0, 2, 3).reshape(MLA_HEADS, Ns, MLA_KV_LORA)

        def pad_rows(a):
            a = a.reshape(DB, T, a.shape[-1])
            return jnp.pad(a, ((0, 0), (0, LP - T), (0, 0))).reshape(DB * LP, a.shape[-1])

        mq_p = pad_rows(zs[:, MQ_OFF:MQ_OFF + hw])
        mk_p = pad_rows(zs[:, MK_OFF:MK_OFF + hw])
        mv_p = pad_rows(zs[:, MV_OFF:MV_OFF + hw])
        sm_p = pad_rows(sm_s)
        kt_s = mk_p.reshape(DB, LP, MLSTM_HEADS, MLSTM_DK).transpose(0, 2, 3, 1)
        smt_s = sm_p[:, :16].reshape(DB, LP, 16).transpose(0, 2, 1)
        state = (state_mlstm_C[l], state_mlstm_n[l][:, :, None, :],
                 jnp.broadcast_to(state_mlstm_m[l][:, :, None, None], (DB, MLSTM_HEADS, 1, LANES)))
        h_s, C_s, n_s, m_s = mlstm_scan(mq_p, mk_p, mv_p, (0, 0, 0), kt_s, sm_p, smt_s, state, DB, 1, LP, T)
        h_ml_s = h_s.reshape(DB, LP, hw)[:, :T].reshape(Ns, hw)
        x1s, xn2s = merge_block(xs, mod_s[2], mod_s[3], mod_s[4], o_fox_s, h_ml_s, zs, o_lat_s,
                                wbf, wbm, w_uv_bd, wbc, wout, g_norm_ffn[l], Ns, SAMPLE_TM)
        ys = _peer_ffn(xn2s, lw, SAMPLE_TM)
        xs = residual_out(x1s, mod_s[5], ys, g_final, Ns, SAMPLE_TM, last)
        sample_rows.append((
            fk_s.reshape(DB, T, H, FOX_HEAD_DIM), fv_s.reshape(DB, T, H, FOX_HEAD_DIM),
            lf_s.reshape(DB, T, H), ckvn_s.reshape(DB, T, MLA_KV_LORA), kr_s.reshape(DB, T, MLA_ROPE),
            C_s, n_s[:, :, 0, :], m_s[:, :, 0, 0]))

    p_out = [jnp.stack(r) for r in zip(*prompt_rows)]
    s_out = [jnp.stack(r) for r in zip(*sample_rows)]
    return (xp.reshape(B, S, D), xs.reshape(DB, T, D), *p_out, *s_out)
```
